```python
import math
import jax, jax.numpy as jnp
from jax import lax
import numpy as np

D_MODEL = 2048
BATCH = 4
SEQ = 4096
DEPTH = 4

N_A = DEPTH // 2
N_B = DEPTH - N_A
GDN_QK_HEADS = 16
GDN_V_HEADS = 32
GDN_HEAD_DIM = 128
GDN_CONV = 4
GDN_CHUNK = 64
GDN_QK_DIM = GDN_QK_HEADS * GDN_HEAD_DIM
GDN_V_DIM = GDN_V_HEADS * GDN_HEAD_DIM
GDN_CONV_DIM = 2 * GDN_QK_DIM + GDN_V_DIM
GDN_IN_DIM = GDN_CONV_DIM + GDN_V_DIM + 2 * GDN_V_HEADS
DIFF_HEADS = D_MODEL // 256
DIFF_QK_DIM = 128
DIFF_V_DIM = 2 * DIFF_QK_DIM
DIFF_Q_WIDTH = DIFF_HEADS * 2 * DIFF_QK_DIM
DIFF_KV_WIDTH = DIFF_Q_WIDTH + DIFF_HEADS * DIFF_V_DIM
Q_BLOCK = 128
D_FF = 4 * D_MODEL
PLE_DIM = 256
EPS = 1e-6

kernel_name = "yoco_gdn_diffattn_hybrid"


def rmsnorm(x, w, eps=EPS):
    xf = x.astype(jnp.float32)
    y = xf * lax.rsqrt(jnp.mean(xf * xf, axis=-1, keepdims=True) + eps)
    return (y * w.astype(jnp.float32)).astype(x.dtype)


def l2norm(x, eps=1e-6):
    xf = x.astype(jnp.float32)
    return xf * lax.rsqrt(jnp.sum(xf * xf, axis=-1, keepdims=True) + eps)


def causal_depthwise_conv(x, w):
    return lax.conv_general_dilated(
        x, w[:, None, :].astype(x.dtype), window_strides=(1,),
        padding=[(GDN_CONV - 1, 0)],
        dimension_numbers=('NWC', 'WIO', 'NWC'),
        feature_group_count=x.shape[-1])


def gated_delta_rule(q, k, v, g, beta):
    B, T, H, dk = q.shape
    dv = v.shape[-1]
    C = GDN_CHUNK
    N = T // C

    def to_chunks(a):
        return a.reshape(B, N, C, H, a.shape[-1]).transpose(1, 0, 3, 2, 4)

    q, k, v = to_chunks(q), to_chunks(k), to_chunks(v)
    g = g.reshape(B, N, C, H).transpose(1, 0, 3, 2)
    beta = beta.reshape(B, N, C, H).transpose(1, 0, 3, 2)
    gc = jnp.cumsum(g, axis=-1)
    causal = jnp.tril(jnp.ones((C, C), dtype=bool))
    strict = jnp.tril(jnp.ones((C, C), dtype=bool), k=-1)
    L = jnp.exp(jnp.where(causal, gc[..., :, None] - gc[..., None, :], -jnp.inf))
    kb = k * beta[..., None]
    A = jnp.where(strict, jnp.einsum('nbhid,nbhjd->nbhij', kb, k) * L, 0.0)
    eye = jnp.eye(C, dtype=jnp.float32)
    rhs = jnp.concatenate([v * beta[..., None], kb * jnp.exp(gc)[..., None]], axis=-1)
    sol = lax.linalg.triangular_solve(A + eye, rhs, left_side=True, lower=True, unit_diagonal=True)
    u, w = sol[..., :dv], sol[..., dv:]
    qk = jnp.where(causal, jnp.einsum('nbhid,nbhjd->nbhij', q, k) * L, 0.0)
    q_dec = q * jnp.exp(gc)[..., None]
    k_dec = k * jnp.exp(gc[..., -1:] - gc)[..., None]
    g_last = jnp.exp(gc[..., -1])

    def step(S, xs):
        qd, kd, u_c, w_c, qk_c, gl = xs
        v_new = u_c - jnp.einsum('bhcd,bhde->bhce', w_c, S)
        o = jnp.einsum('bhcd,bhde->bhce', qd, S) + jnp.einsum('bhij,bhje->bhie', qk_c, v_new)
        S = S * gl[..., None, None] + jnp.einsum('bhcd,bhce->bhde', kd, v_new)
        return S, o

    S0 = jnp.zeros((B, H, dk, dv), jnp.float32)
    _, o = lax.scan(step, S0, (q_dec, k_dec, u, w, qk, g_last))
    return o.transpose(1, 0, 3, 2, 4).reshape(B, T, H, dv)


def gated_deltanet(xn, w_in, conv_w, a_log, dt_bias, norm_w, w_out):
    B, T, _ = xn.shape
    proj = xn @ w_in
    s1 = GDN_CONV_DIM
    s2 = s1 + GDN_V_DIM
    s3 = s2 + GDN_V_HEADS
    qkv, z, b, a = proj[..., :s1], proj[..., s1:s2], proj[..., s2:s3], proj[..., s3:]
    qkv = jax.nn.silu(causal_depthwise_conv(qkv, conv_w))
    q = qkv[..., :GDN_QK_DIM].reshape(B, T, GDN_QK_HEADS, GDN_HEAD_DIM)
    k = qkv[..., GDN_QK_DIM:2 * GDN_QK_DIM].reshape(B, T, GDN_QK_HEADS, GDN_HEAD_DIM)
    v = qkv[..., 2 * GDN_QK_DIM:].reshape(B, T, GDN_V_HEADS, GDN_HEAD_DIM).astype(jnp.float32)
    rep = GDN_V_HEADS // GDN_QK_HEADS
    q = jnp.repeat(l2norm(q), rep, axis=2) * (GDN_HEAD_DIM ** -0.5)
    k = jnp.repeat(l2norm(k), rep, axis=2)
    beta = jax.nn.sigmoid(b.astype(jnp.float32))
    g = -jnp.exp(a_log.astype(jnp.float32)) * jax.nn.softplus(a.astype(jnp.float32) + dt_bias.astype(jnp.float32))
    o = gated_delta_rule(q, k, v, g, beta)
    o = rmsnorm(o, norm_w) * jax.nn.silu(z.reshape(B, T, GDN_V_HEADS, GDN_HEAD_DIM).astype(jnp.float32))
    return o.reshape(B, T, GDN_V_DIM).astype(xn.dtype) @ w_out


def alibi_slopes(n_heads):
    return jnp.exp2(-8.0 * jnp.arange(1, n_heads + 1, dtype=jnp.float32) / n_heads)


def diff_attention(xn, k, v, w_q, lq1, lk1, lq2, lk2, subln_w, w_o, layer_idx):
    B, T, _ = xn.shape
    q = (xn @ w_q).reshape(B, T, DIFF_HEADS, 2, DIFF_QK_DIM) * (DIFF_QK_DIM ** -0.5)
    lam_init = 0.8 - 0.6 * math.exp(-0.3 * layer_idx)
    lam = (jnp.exp(jnp.sum(lq1.astype(jnp.float32) * lk1.astype(jnp.float32)))
           - jnp.exp(jnp.sum(lq2.astype(jnp.float32) * lk2.astype(jnp.float32))) + lam_init)
    slopes = alibi_slopes(DIFF_HEADS)
    key_pos = jnp.arange(T)

    def block(i):
        start = i * Q_BLOCK
        qb = lax.dynamic_slice_in_dim(q, start, Q_BLOCK, axis=1)
        scores = jnp.einsum('bqhcd,bkhcd->bhcqk', qb, k, preferred_element_type=jnp.float32)
        dist = (start + jnp.arange(Q_BLOCK))[:, None] - key_pos[None, :]
        bias = jnp.where(dist >= 0, -slopes[:, None, None] * dist.astype(jnp.float32), -jnp.inf)
        probs = jax.nn.softmax(scores + bias[None, :, None], axis=-1)
        attn = probs[:, :, 0] - lam * probs[:, :, 1]
        return jnp.einsum('bhqk,bkhe->bqhe', attn.astype(v.dtype), v)

    o = lax.map(block, jnp.arange(T // Q_BLOCK))
    o = o.transpose(1, 0, 2, 3, 4).reshape(B, T, DIFF_HEADS, DIFF_V_DIM)
    o = rmsnorm(o, subln_w) * (1.0 - lam_init)
    return o.reshape(B, T, DIFF_HEADS * DIFF_V_DIM) @ w_o


def setup_inputs(seed: int = 0) -> dict:
    key = jax.random.key(seed)
    ks = jax.random.split(key, 26)
    f32 = jnp.float32

    def dense(k, shape, fan_in):
        return jax.random.normal(k, shape, f32) * (fan_in ** -0.5)

    def gain(k, shape):
        return 1.0 + 0.01 * jax.random.normal(k, shape, f32)

    x = jax.random.normal(ks[0], (BATCH, SEQ, D_MODEL), f32)
    p = jax.random.normal(ks[1], (DEPTH, BATCH, SEQ, PLE_DIM), f32)
    norm_mix = gain(ks[2], (DEPTH, D_MODEL))
    norm_mlp = gain(ks[3], (DEPTH, D_MODEL))
    norm_ple = gain(ks[4], (DEPTH, D_MODEL))
    gdn_w_in = dense(ks[5], (N_A, D_MODEL, GDN_IN_DIM), D_MODEL)
    gdn_conv_w = jax.random.normal(ks[6], (N_A, GDN_CONV, GDN_CONV_DIM), f32) * (GDN_CONV ** -0.5)
    gdn_a_log = jnp.log(jax.random.uniform(ks[7], (N_A, GDN_V_HEADS), f32, 1.0, 16.0))
    dt = jnp.exp(jax.random.uniform(ks[8], (N_A, GDN_V_HEADS), f32, math.log(1e-3), math.log(1e-1)))
    gdn_dt_bias = dt + jnp.log(-jnp.expm1(-dt))
    gdn_norm_w = gain(ks[9], (N_A, GDN_HEAD_DIM))
    gdn_w_out = dense(ks[10], (N_A, GDN_V_DIM, D_MODEL), GDN_V_DIM)
    kv_norm = gain(ks[11], (D_MODEL,))
    w_kv = dense(ks[12], (D_MODEL, DIFF_KV_WIDTH), D_MODEL)
    diff_w_q = dense(ks[13], (N_B, D_MODEL, DIFF_Q_WIDTH), D_MODEL)
    diff_lambda_q1 = 0.1 * jax.random.normal(ks[14], (N_B, DIFF_QK_DIM), f32)
    diff_lambda_k1 = 0.1 * jax.random.normal(ks[15], (N_B, DIFF_QK_DIM), f32)
    diff_lambda_q2 = 0.1 * jax.random.normal(ks[16], (N_B, DIFF_QK_DIM), f32)
    diff_lambda_k2 = 0.1 * jax.random.normal(ks[17], (N_B, DIFF_QK_DIM), f32)
    diff_subln_w = gain(ks[18], (N_B, DIFF_V_DIM))
    diff_w_o = dense(ks[19], (N_B, DIFF_HEADS * DIFF_V_DIM, D_MODEL), DIFF_HEADS * DIFF_V_DIM)
    mlp_w1 = dense(ks[20], (DEPTH, D_MODEL, D_FF), D_MODEL)
    mlp_w2 = dense(ks[21], (DEPTH, D_FF, D_MODEL), D_FF)
    ple_w_proj = dense(ks[22], (DEPTH, PLE_DIM, D_MODEL), PLE_DIM)
    ple_w_gate = dense(ks[23], (DEPTH, D_MODEL, D_MODEL), D_MODEL)
    final_norm = gain(ks[24], (D_MODEL,))
    return {
        'x': x, 'p': p, 'norm_mix': norm_mix, 'norm_mlp': norm_mlp, 'norm_ple': norm_ple,
        'gdn_w_in': gdn_w_in, 'gdn_conv_w': gdn_conv_w, 'gdn_a_log': gdn_a_log,
        'gdn_dt_bias': gdn_dt_bias, 'gdn_norm_w': gdn_norm_w, 'gdn_w_out': gdn_w_out,
        'kv_norm': kv_norm, 'w_kv': w_kv, 'diff_w_q': diff_w_q,
        'diff_lambda_q1': diff_lambda_q1, 'diff_lambda_k1': diff_lambda_k1,
        'diff_lambda_q2': diff_lambda_q2, 'diff_lambda_k2': diff_lambda_k2,
        'diff_subln_w': diff_subln_w, 'diff_w_o': diff_w_o,
        'mlp_w1': mlp_w1, 'mlp_w2': mlp_w2, 'ple_w_proj': ple_w_proj, 'ple_w_gate': ple_w_gate,
        'final_norm': final_norm,
    }


def reference(x, p, norm_mix, norm_mlp, norm_ple, gdn_w_in, gdn_conv_w, gdn_a_log, gdn_dt_bias,
              gdn_norm_w, gdn_w_out, kv_norm, w_kv, diff_w_q, diff_lambda_q1, diff_lambda_k1,
              diff_lambda_q2, diff_lambda_k2, diff_subln_w, diff_w_o, mlp_w1, mlp_w2,
              ple_w_proj, ple_w_gate, final_norm):
    B, T, _ = x.shape
    h = x
    k_shared = None
    v_shared = None
    for i in range(DEPTH):
        xn = rmsnorm(h, norm_mix[i])
        if i < N_A:
            h = h + gated_deltanet(xn, gdn_w_in[i], gdn_conv_w[i], gdn_a_log[i], gdn_dt_bias[i],
                                   gdn_norm_w[i], gdn_w_out[i])
        else:
            j = i - N_A
            h = h + diff_attention(xn, k_shared, v_shared, diff_w_q[j], diff_lambda_q1[j],
                                   diff_lambda_k1[j], diff_lambda_q2[j], diff_lambda_k2[j],
                                   diff_subln_w[j], diff_w_o[j], i)
        hn = rmsnorm(h, norm_mlp[i])
        h = h + jnp.square(jax.nn.relu(hn @ mlp_w1[i])) @ mlp_w2[i]
        gate = jax.nn.sigmoid(rmsnorm(h, norm_ple[i]) @ ple_w_gate[i])
        h = h + (p[i] @ ple_w_proj[i]) * gate
        if i == N_A - 1:
            kv = rmsnorm(h, kv_norm) @ w_kv
            k_shared = kv[..., :DIFF_Q_WIDTH].reshape(B, T, DIFF_HEADS, 2, DIFF_QK_DIM)
            v_shared = kv[..., DIFF_Q_WIDTH:].reshape(B, T, DIFF_HEADS, DIFF_V_DIM)
    return rmsnorm(h, final_norm)
```

```python
import functools
import math

import numpy as np
import jax
import jax.numpy as jnp
from jax import lax
from jax.experimental import pallas as pl
from jax.experimental.pallas import tpu as pltpu

F32 = jnp.float32
BF16 = jnp.bfloat16
EPS = 1e-6
LANES = 128
HEAD = 128
CHUNK = 64
CONV_K = 4
HALO = 8
NEG = -1e30
VMEM_LIMIT = 56 * 1024 * 1024
HI = lax.Precision.HIGHEST


def _cparams(sem):
    return pltpu.CompilerParams(dimension_semantics=sem, vmem_limit_bytes=VMEM_LIMIT)


def _tile(n, target):
    if n <= target:
        return n
    t = target - target % LANES
    while n % t:
        t -= LANES
    assert t > 0
    return t


def _dot(a, b):
    return jnp.dot(a, b, preferred_element_type=F32)


def _dot_nt(a, b):
    return lax.dot_general(a, b, (((1,), (1,)), ((), ())), preferred_element_type=F32)


def _dot_tn(a, b):
    return lax.dot_general(a, b, (((0,), (0,)), ((), ())), preferred_element_type=F32)


def _silu(x):
    return x * jax.nn.sigmoid(x)


def _norm_matmul_kernel(x_ref, g_ref, w_ref, o_ref, xn_ref, *, act, scale):
    @pl.when(pl.program_id(1) == 0)
    def _():
        x = x_ref[...]
        ms = jnp.mean(x * x, axis=-1, keepdims=True)
        xn_ref[...] = (x * lax.rsqrt(ms + EPS) * g_ref[...]).astype(BF16)

    acc = _dot(xn_ref[...], w_ref[...])
    if act == "relu2":
        acc = jnp.square(jnp.maximum(acc, 0.0))
    if scale != 1.0:
        acc = acc * scale
    o_ref[...] = acc.astype(o_ref.dtype)


def _norm_matmul(x, gain, w, *, act=None, scale=1.0, out_dtype=BF16, tm=1024, tn=1024):
    m, d = x.shape
    n = w.shape[1]
    tm, tn = _tile(m, tm), _tile(n, tn)
    return pl.pallas_call(
        functools.partial(_norm_matmul_kernel, act=act, scale=scale),
        grid=(m // tm, n // tn),
        in_specs=[pl.BlockSpec((tm, d), lambda i, j: (i, 0)),
                  pl.BlockSpec((1, d), lambda i, j: (0, 0)),
                  pl.BlockSpec((d, tn), lambda i, j: (0, j))],
        out_specs=pl.BlockSpec((tm, tn), lambda i, j: (i, j)),
        out_shape=jax.ShapeDtypeStruct((m, n), out_dtype),
        scratch_shapes=[pltpu.VMEM((tm, d), BF16)],
        compiler_params=_cparams(("parallel", "arbitrary")),
        name="norm_matmul",
    )(x, gain.reshape(1, d), w)


def _matmul_residual_kernel(a_ref, w_ref, h_ref, o_ref):
    @pl.when(pl.program_id(2) == 0)
    def _():
        o_ref[...] = h_ref[...]

    o_ref[...] += _dot(a_ref[...], w_ref[...])


def _matmul_residual(a, w, h, *, tm=1024, tn=1024, tk=2048):
    m, k = a.shape
    n = w.shape[1]
    tm, tn, tk = _tile(m, tm), _tile(n, tn), _tile(k, tk)
    return pl.pallas_call(
        _matmul_residual_kernel,
        grid=(m // tm, n // tn, k // tk),
        in_specs=[pl.BlockSpec((tm, tk), lambda i, j, kk: (i, kk)),
                  pl.BlockSpec((tk, tn), lambda i, j, kk: (kk, j)),
                  pl.BlockSpec((tm, tn), lambda i, j, kk: (i, j))],
        out_specs=pl.BlockSpec((tm, tn), lambda i, j, kk: (i, j)),
        out_shape=jax.ShapeDtypeStruct((m, n), F32),
        compiler_params=_cparams(("parallel", "parallel", "arbitrary")),
        name="matmul_residual",
    )(a, w, h)


def _ple_kernel(h_ref, g_ref, wg_ref, p_ref, wp_ref, o_ref):
    h = h_ref[...]
    ms = jnp.mean(h * h, axis=-1, keepdims=True)
    hn = (h * lax.rsqrt(ms + EPS) * g_ref[...]).astype(BF16)
    gate = jax.nn.sigmoid(_dot(hn, wg_ref[...]))
    emb = _dot(p_ref[...].astype(BF16), wp_ref[...])
    o_ref[...] = h + emb * gate


def _ple(h, gain, w_gate, p, w_proj, *, tm=512):
    m, d = h.shape
    e = p.shape[1]
    tm = min(tm, m)
    assert m % tm == 0
    return pl.pallas_call(
        _ple_kernel,
        grid=(m // tm,),
        in_specs=[pl.BlockSpec((tm, d), lambda i: (i, 0)),
                  pl.BlockSpec((1, d), lambda i: (0, 0)),
                  pl.BlockSpec((d, d), lambda i: (0, 0)),
                  pl.BlockSpec((tm, e), lambda i: (i, 0)),
                  pl.BlockSpec((e, d), lambda i: (0, 0))],
        out_specs=pl.BlockSpec((tm, d), lambda i: (i, 0)),
        out_shape=jax.ShapeDtypeStruct((m, d), F32),
        compiler_params=_cparams(("parallel",)),
        name="ple",
    )(h, gain.reshape(1, d), w_gate, p, w_proj)


def _rmsnorm_kernel(x_ref, g_ref, o_ref):
    x = x_ref[...]
    ms = jnp.mean(x * x, axis=-1, keepdims=True)
    o_ref[...] = x * lax.rsqrt(ms + EPS) * g_ref[...]


def _rmsnorm(x, gain, *, tm=1024):
    m, d = x.shape
    tm = min(tm, m)
    assert m % tm == 0
    return pl.pallas_call(
        _rmsnorm_kernel,
        grid=(m // tm,),
        in_specs=[pl.BlockSpec((tm, d), lambda i: (i, 0)),
                  pl.BlockSpec((1, d), lambda i: (0, 0))],
        out_specs=pl.BlockSpec((tm, d), lambda i: (i, 0)),
        out_shape=jax.ShapeDtypeStruct((m, d), F32),
        compiler_params=_cparams(("parallel",)),
        name="final_rmsnorm",
    )(x, gain.reshape(1, d))


def _gdn_gates_kernel(ba_ref, alog_ref, dt_ref, cols_ref, rows_ref, *, n_heads):
    ba = ba_ref[0]
    tt = ba.shape[0]
    beta = jax.nn.sigmoid(ba)
    g = -jnp.exp(alog_ref[...]) * jax.nn.softplus(ba + dt_ref[...])
    row = lax.broadcasted_iota(jnp.int32, (tt, tt), 0)
    col = lax.broadcasted_iota(jnp.int32, (tt, tt), 1)
    tri = jnp.where((row // CHUNK == col // CHUNK) & (col <= row), 1.0, 0.0).astype(F32)
    gc = jnp.dot(tri, g, preferred_element_type=F32, precision=HI)
    lane = lax.broadcasted_iota(jnp.int32, ba.shape, 1)
    cols = jnp.where(lane < n_heads, beta, gc)
    cols_ref[0] = cols
    rows_ref[0] = cols.T


def _gdn_gates(ba, a_log, dt_bias, *, tt=256):
    b, t, _ = ba.shape
    hv = a_log.shape[0]
    assert 2 * hv <= LANES and t % tt == 0 and tt % CHUNK == 0
    pad = lambda v: jnp.zeros((1, LANES), F32).at[0, hv:2 * hv].set(v.astype(F32))
    return pl.pallas_call(
        functools.partial(_gdn_gates_kernel, n_heads=hv),
        grid=(b, t // tt),
        in_specs=[pl.BlockSpec((1, tt, LANES), lambda bi, i: (bi, i, 0)),
                  pl.BlockSpec((1, LANES), lambda bi, i: (0, 0)),
                  pl.BlockSpec((1, LANES), lambda bi, i: (0, 0))],
        out_specs=[pl.BlockSpec((1, tt, LANES), lambda bi, i: (bi, i, 0)),
                   pl.BlockSpec((1, LANES, tt), lambda bi, i: (bi, 0, i))],
        out_shape=[jax.ShapeDtypeStruct((b, t, LANES), F32),
                   jax.ShapeDtypeStruct((b, LANES, t), F32)],
        compiler_params=_cparams(("parallel", "parallel")),
        name="gdn_gates",
    )(ba, pad(a_log), pad(dt_bias))


def _gdn_core_kernel(q_ref, k_ref, v_ref, z_ref, wq_ref, wk_ref, wv_ref, gcol_ref, grow_ref, nw_ref,
                     o_ref, s_ref, qbuf, kbuf, vbuf, *, n_vheads):
    tt = q_ref.shape[1]
    hq = pl.program_id(1)

    @pl.when(pl.program_id(2) == 0)
    def _():
        s_ref[...] = jnp.zeros_like(s_ref)
        qbuf[0:HALO, :] = jnp.zeros((HALO, qbuf.shape[1]), F32)
        kbuf[0:HALO, :] = jnp.zeros((HALO, kbuf.shape[1]), F32)
        vbuf[0:HALO, :] = jnp.zeros((HALO, vbuf.shape[1]), F32)

    def conv_silu(x_ref, w_ref, buf):
        buf[HALO:HALO + tt, :] = x_ref[0].astype(F32)
        w = w_ref[...]
        acc = w[0:1, :] * buf[pl.ds(HALO - (CONV_K - 1), tt), :]
        for j in range(1, CONV_K):
            acc = acc + w[j:j + 1, :] * buf[pl.ds(HALO - (CONV_K - 1) + j, tt), :]
        buf[0:HALO, :] = buf[tt:tt + HALO, :]
        return _silu(acc)

    def l2norm(x):
        return x * lax.rsqrt(jnp.sum(x * x, axis=-1, keepdims=True) + 1e-6)

    q = l2norm(conv_silu(q_ref, wq_ref, qbuf)) * (HEAD ** -0.5)
    k = l2norm(conv_silu(k_ref, wk_ref, kbuf))
    v = conv_silu(v_ref, wv_ref, vbuf)
    z = z_ref[0].astype(F32)
    gcols = gcol_ref[0]
    nw = nw_ref[...]

    ci = lax.broadcasted_iota(jnp.int32, (CHUNK, CHUNK), 0)
    cj = lax.broadcasted_iota(jnp.int32, (CHUNK, CHUNK), 1)
    causal = cj <= ci
    strict = cj < ci
    eye = jnp.where(ci == cj, 1.0, 0.0).astype(F32)
    sel_r = lax.broadcasted_iota(jnp.int32, (LANES, LANES), 0)

    for e in range(2):
        hv = 2 * hq + e
        beta_b = jnp.dot(gcols, jnp.where(sel_r == hv, 1.0, 0.0).astype(F32),
                         preferred_element_type=F32, precision=HI)
        gc_b = jnp.dot(gcols, jnp.where(sel_r == n_vheads + hv, 1.0, 0.0).astype(F32),
                       preferred_element_type=F32, precision=HI)
        gc_row = grow_ref[0, pl.ds(n_vheads + hv, 1), :]
        ve = v[:, e * HEAD:(e + 1) * HEAD]
        ze = z[:, e * HEAD:(e + 1) * HEAD]
        s = s_ref[e]
        for c in range(tt // CHUNK):
            r = slice(c * CHUNK, (c + 1) * CHUNK)
            kc, qc, vc = k[r], q[r], ve[r]
            bcol, gcol = beta_b[r], gc_b[r]
            diff = gcol[:, :CHUNK] - gc_row[:, r]
            decay = jnp.exp(jnp.where(causal, diff, NEG))
            kb = kc * bcol
            kc16 = kc.astype(BF16)
            a = jnp.where(strict, _dot_nt(kb.astype(BF16), kc16) * decay, 0.0)
            qk = jnp.where(causal, _dot_nt(qc.astype(BF16), kc16) * decay, 0.0)
            p = -a
            tinv = eye + p
            for _ in range(int(math.log2(CHUNK)) - 1):
                p16 = p.astype(BF16)
                p = _dot(p16, p16)
                tinv = tinv + _dot(tinv.astype(BF16), p.astype(BF16))
            eg = jnp.exp(gcol)
            rhs = jnp.concatenate([vc * bcol, kb * eg], axis=1)
            sol = _dot(tinv.astype(BF16), rhs.astype(BF16))
            u, w = sol[:, :HEAD], sol[:, HEAD:]
            glast = gcol[CHUNK - 1:CHUNK, :]
            k_dec = kc * jnp.exp(glast - gcol)
            q_dec = qc * eg
            s16 = s.astype(BF16)
            v_new = u - _dot(w.astype(BF16), s16)
            o = _dot(q_dec.astype(BF16), s16) + _dot(qk.astype(BF16), v_new.astype(BF16))
            s = s * jnp.exp(glast) + _dot_tn(k_dec.astype(BF16), v_new.astype(BF16))
            on = o * lax.rsqrt(jnp.mean(o * o, axis=-1, keepdims=True) + EPS) * nw
            o_ref[0, r, e * HEAD:(e + 1) * HEAD] = (on * _silu(ze[r])).astype(o_ref.dtype)
        s_ref[e] = s


def _gdn_core(proj, conv_w, gcols, grows, norm_w, *, n_qheads, n_vheads, tt=256):
    b, t, _ = proj.shape
    assert n_vheads == 2 * n_qheads and t % tt == 0 and tt % CHUNK == 0
    qk_dim = n_qheads * HEAD
    v_dim = n_vheads * HEAD
    v_blk0 = (2 * qk_dim) // (2 * HEAD)
    z_blk0 = (2 * qk_dim + v_dim) // (2 * HEAD)
    return pl.pallas_call(
        functools.partial(_gdn_core_kernel, n_vheads=n_vheads),
        grid=(b, n_qheads, t // tt),
        in_specs=[pl.BlockSpec((1, tt, HEAD), lambda bi, h, ti: (bi, ti, h)),
                  pl.BlockSpec((1, tt, HEAD), lambda bi, h, ti: (bi, ti, n_qheads + h)),
                  pl.BlockSpec((1, tt, 2 * HEAD), lambda bi, h, ti: (bi, ti, v_blk0 + h)),
                  pl.BlockSpec((1, tt, 2 * HEAD), lambda bi, h, ti: (bi, ti, z_blk0 + h)),
                  pl.BlockSpec((CONV_K, HEAD), lambda bi, h, ti: (0, h)),
                  pl.BlockSpec((CONV_K, HEAD), lambda bi, h, ti: (0, n_qheads + h)),
                  pl.BlockSpec((CONV_K, 2 * HEAD), lambda bi, h, ti: (0, v_blk0 + h)),
                  pl.BlockSpec((1, tt, LANES), lambda bi, h, ti: (bi, ti, 0)),
                  pl.BlockSpec((1, LANES, tt), lambda bi, h, ti: (bi, 0, ti)),
                  pl.BlockSpec((1, HEAD), lambda bi, h, ti: (0, 0))],
        out_specs=pl.BlockSpec((1, tt, 2 * HEAD), lambda bi, h, ti: (bi, ti, h)),
        out_shape=jax.ShapeDtypeStruct((b, t, v_dim), BF16),
        scratch_shapes=[pltpu.VMEM((2, HEAD, HEAD), F32),
                        pltpu.VMEM((tt + HALO, HEAD), F32),
                        pltpu.VMEM((tt + HALO, HEAD), F32),
                        pltpu.VMEM((tt + HALO, 2 * HEAD), F32)],
        compiler_params=_cparams(("parallel", "parallel", "arbitrary")),
        name="gdn_core",
    )(proj, proj, proj, proj, conv_w, conv_w, conv_w, gcols, grows, norm_w.reshape(1, HEAD))


def _diff_attn_kernel(qi_tab, ki_tab, q1_ref, q2_ref, k1_ref, k2_ref, v_ref, lam_ref, sw_ref, o_ref,
                      m_ref, l_ref, acc_ref, *, n_heads, out_scale, lam_init):
    step = pl.program_id(2)
    h = pl.program_id(1)
    qi = qi_tab[step]
    ki = ki_tab[step]
    tq = q1_ref.shape[1]
    tk = k1_ref.shape[1]

    @pl.when(ki == 0)
    def _():
        m_ref[...] = jnp.full_like(m_ref, NEG)
        l_ref[...] = jnp.zeros_like(l_ref)
        acc_ref[...] = jnp.zeros_like(acc_ref)

    slope = jnp.exp2(-8.0 * (h + 1).astype(F32) / n_heads)
    row = lax.broadcasted_iota(jnp.int32, (tq, tk), 0)
    col = lax.broadcasted_iota(jnp.int32, (tq, tk), 1)
    off = ki * tk - qi * tq
    kpos = (lax.broadcasted_iota(jnp.int32, (1, tk), 1) + off).astype(F32)
    bias = slope * kpos
    visible = (col + off) <= row
    v16 = v_ref[0]

    def one_map(c, q_ref, k_ref):
        s = _dot_nt(q_ref[0], k_ref[0]) + bias
        s = jnp.where(visible, s, NEG)
        m_old = m_ref[c]
        m_new = jnp.maximum(m_old, jnp.max(s, axis=-1, keepdims=True))
        p = jnp.exp(s - m_new)
        alpha = jnp.exp(m_old - m_new)
        l_ref[c] = alpha * l_ref[c] + jnp.sum(p, axis=-1, keepdims=True)
        acc_ref[c] = alpha * acc_ref[c] + _dot(p.astype(BF16), v16)
        m_ref[c] = m_new

    one_map(0, q1_ref, k1_ref)
    one_map(1, q2_ref, k2_ref)

    @pl.when((ki + 1) * tk >= (qi + 1) * tq)
    def _():
        lq1, lk1, lq2, lk2 = lam_ref[0:1, :], lam_ref[1:2, :], lam_ref[2:3, :], lam_ref[3:4, :]
        lam = (jnp.exp(jnp.sum(lq1 * lk1, axis=-1, keepdims=True))
               - jnp.exp(jnp.sum(lq2 * lk2, axis=-1, keepdims=True)) + lam_init)
        o = acc_ref[0] / l_ref[0] - lam * (acc_ref[1] / l_ref[1])
        on = o * lax.rsqrt(jnp.mean(o * o, axis=-1, keepdims=True) + EPS) * sw_ref[...]
        o_ref[0] = (on * out_scale).astype(o_ref.dtype)


def _diff_attn(q, kv, lam_vecs, subln_w, *, n_heads, layer_idx, tq=512, tk=512):
    b, t, qw = q.shape
    tq, tk = min(tq, t), min(tk, t)
    assert t % tq == 0 and tq % tk == 0
    dv = 2 * HEAD
    v_blk0 = qw // dv
    pairs = [(i, j) for i in range(t // tq) for j in range(((i + 1) * tq) // tk)]
    qi_tab = jnp.asarray(np.array([p[0] for p in pairs], np.int32))
    ki_tab = jnp.asarray(np.array([p[1] for p in pairs], np.int32))
    lam_init = 0.8 - 0.6 * math.exp(-0.3 * layer_idx)
    grid_spec = pltpu.PrefetchScalarGridSpec(
        num_scalar_prefetch=2,
        grid=(b, n_heads, len(pairs)),
        in_specs=[pl.BlockSpec((1, tq, HEAD), lambda bi, h, s, qt, kt: (bi, qt[s], 2 * h)),
                  pl.BlockSpec((1, tq, HEAD), lambda bi, h, s, qt, kt: (bi, qt[s], 2 * h + 1)),
                  pl.BlockSpec((1, tk, HEAD), lambda bi, h, s, qt, kt: (bi, kt[s], 2 * h)),
                  pl.BlockSpec((1, tk, HEAD), lambda bi, h, s, qt, kt: (bi, kt[s], 2 * h + 1)),
                  pl.BlockSpec((1, tk, dv), lambda bi, h, s, qt, kt: (bi, kt[s], v_blk0 + h)),
                  pl.BlockSpec((4, HEAD), lambda bi, h, s, qt, kt: (0, 0)),
                  pl.BlockSpec((1, dv), lambda bi, h, s, qt, kt: (0, 0))],
        out_specs=pl.BlockSpec((1, tq, dv), lambda bi, h, s, qt, kt: (bi, qt[s], h)),
        scratch_shapes=[pltpu.VMEM((2, tq, 1), F32),
                        pltpu.VMEM((2, tq, 1), F32),
                        pltpu.VMEM((2, tq, dv), F32)])
    return pl.pallas_call(
        functools.partial(_diff_attn_kernel, n_heads=n_heads, out_scale=1.0 - lam_init, lam_init=lam_init),
        grid_spec=grid_spec,
        out_shape=jax.ShapeDtypeStruct((b, t, n_heads * dv), BF16),
        compiler_params=_cparams(("parallel", "parallel", "arbitrary")),
        name="diff_attn",
    )(qi_tab, ki_tab, q, q, kv, kv, kv, lam_vecs, subln_w.reshape(1, dv))


def kernel(x, p, norm_mix, norm_mlp, norm_ple, gdn_w_in, gdn_conv_w, gdn_a_log, gdn_dt_bias, gdn_norm_w, gdn_w_out, kv_norm, w_kv, diff_w_q, diff_lambda_q1, diff_lambda_k1, diff_lambda_q2, diff_lambda_k2, diff_subln_w, diff_w_o, mlp_w1, mlp_w2, ple_w_proj, ple_w_gate, final_norm):
    b, t, d = x.shape
    m = b * t
    depth = norm_mix.shape[0]
    n_a = gdn_w_in.shape[0]
    n_vheads = gdn_a_log.shape[1]
    v_dim = n_vheads * HEAD
    conv_dim = gdn_conv_w.shape[2]
    qk_dim = (conv_dim - v_dim) // 2
    n_qheads = qk_dim // HEAD
    main_w = conv_dim + v_dim
    n_dheads = diff_w_q.shape[2] // (2 * HEAD)

    h = x.reshape(m, d)
    kv = None
    for i in range(depth):
        if i < n_a:
            w_in = gdn_w_in[i]
            w_main = w_in[:, :main_w].astype(BF16)
            w_ba = jnp.pad(w_in[:, main_w:], ((0, 0), (0, LANES - 2 * n_vheads))).astype(BF16)
            proj = _norm_matmul(h, norm_mix[i], w_main)
            ba = _norm_matmul(h, norm_mix[i], w_ba, out_dtype=F32)
            gcols, grows = _gdn_gates(ba.reshape(b, t, LANES), gdn_a_log[i], gdn_dt_bias[i])
            o = _gdn_core(proj.reshape(b, t, main_w), gdn_conv_w[i], gcols, grows,
                          gdn_norm_w[i], n_qheads=n_qheads, n_vheads=n_vheads)
            h = _matmul_residual(o.reshape(m, v_dim), gdn_w_out[i].astype(BF16), h)
        else:
            j = i - n_a
            q = _norm_matmul(h, norm_mix[i], diff_w_q[j].astype(BF16), scale=HEAD ** -0.5)
            lam_vecs = jnp.stack([diff_lambda_q1[j], diff_lambda_k1[j],
                                  diff_lambda_q2[j], diff_lambda_k2[j]]).astype(F32)
            o = _diff_attn(q.reshape(b, t, -1), kv, lam_vecs, diff_subln_w[j],
                           n_heads=n_dheads, layer_idx=i)
            h = _matmul_residual(o.reshape(m, -1), diff_w_o[j].astype(BF16), h)
        a = _norm_matmul(h, norm_mlp[i], mlp_w1[i].astype(BF16), act="relu2")
        h = _matmul_residual(a, mlp_w2[i].astype(BF16), h)
        h = _ple(h, norm_ple[i], ple_w_gate[i].astype(BF16), p[i].reshape(m, -1),
                 ple_w_proj[i].astype(BF16))
        if i == n_a - 1:
            kv = _norm_matmul(h, kv_norm, w_kv.astype(BF16)).reshape(b, t, -1)
    return _rmsnorm(h, final_norm).reshape(b, t, d)
```

```python
import functools
import math

import numpy as np
import jax
import jax.numpy as jnp
from jax import lax
from jax.experimental import pallas as pl
from jax.experimental.pallas import tpu as pltpu

F32 = jnp.float32
BF16 = jnp.bfloat16
EPS = 1e-6
LANES = 128
HEAD = 128
CHUNK = 64
CONV_K = 4
HALO = 8
NEG = -1e30
VMEM_LIMIT = 56 * 1024 * 1024
HI = lax.Precision.HIGHEST


def _cparams(sem):
    return pltpu.CompilerParams(dimension_semantics=sem, vmem_limit_bytes=VMEM_LIMIT)


def _tile(n, target):
    if n <= target:
        return n
    t = target - target % LANES
    while n % t:
        t -= LANES
    assert t > 0
    return t


def _dot(a, b):
    return jnp.dot(a, b, preferred_element_type=F32)


def _dot_nt(a, b):
    return lax.dot_general(a, b, (((1,), (1,)), ((), ())), preferred_element_type=F32)


def _dot_tn(a, b):
    return lax.dot_general(a, b, (((0,), (0,)), ((), ())), preferred_element_type=F32)


def _silu(x):
    return x * jax.nn.sigmoid(x)


def _norm_matmul_kernel(x_ref, g_ref, w_ref, o_ref, xn_ref, *, act, scale):
    @pl.when(pl.program_id(1) == 0)
    def _():
        x = x_ref[...]
        ms = jnp.mean(x * x, axis=-1, keepdims=True)
        xn_ref[...] = (x * lax.rsqrt(ms + EPS) * g_ref[...]).astype(BF16)

    acc = _dot(xn_ref[...], w_ref[...])
    if act == "relu2":
        acc = jnp.square(jnp.maximum(acc, 0.0))
    if scale != 1.0:
        acc = acc * scale
    o_ref[...] = acc.astype(o_ref.dtype)


def _norm_matmul(x, gain, w, *, act=None, scale=1.0, out_dtype=BF16, tm=1024, tn=1024):
    m, d = x.shape
    n = w.shape[1]
    tm, tn = _tile(m, tm), _tile(n, tn)
    return pl.pallas_call(
        functools.partial(_norm_matmul_kernel, act=act, scale=scale),
        grid=(m // tm, n // tn),
        in_specs=[pl.BlockSpec((tm, d), lambda i, j: (i, 0)),
                  pl.BlockSpec((1, d), lambda i, j: (0, 0)),
                  pl.BlockSpec((d, tn), lambda i, j: (0, j))],
        out_specs=pl.BlockSpec((tm, tn), lambda i, j: (i, j)),
        out_shape=jax.ShapeDtypeStruct((m, n), out_dtype),
        scratch_shapes=[pltpu.VMEM((tm, d), BF16)],
        compiler_params=_cparams(("parallel", "arbitrary")),
        name="norm_matmul",
    )(x, gain.reshape(1, d), w)


def _matmul_residual_kernel(a_ref, w_ref, h_ref, o_ref):
    @pl.when(pl.program_id(2) == 0)
    def _():
        o_ref[...] = h_ref[...]

    o_ref[...] += _dot(a_ref[...], w_ref[...])


def _matmul_residual(a, w, h, *, tm=1024, tn=1024, tk=2048):
    m, k = a.shape
    n = w.shape[1]
    tm, tn, tk = _tile(m, tm), _tile(n, tn), _tile(k, tk)
    return pl.pallas_call(
        _matmul_residual_kernel,
        grid=(m // tm, n // tn, k // tk),
        in_specs=[pl.BlockSpec((tm, tk), lambda i, j, kk: (i, kk)),
                  pl.BlockSpec((tk, tn), lambda i, j, kk: (kk, j)),
                  pl.BlockSpec((tm, tn), lambda i, j, kk: (i, j))],
        out_specs=pl.BlockSpec((tm, tn), lambda i, j, kk: (i, j)),
        out_shape=jax.ShapeDtypeStruct((m, n), F32),
        compiler_params=_cparams(("parallel", "parallel", "arbitrary")),
        name="matmul_residual",
    )(a, w, h)


def _ple_kernel(h_ref, g_ref, wg_ref, p_ref, wp_ref, o_ref):
    h = h_ref[...]
    ms = jnp.mean(h * h, axis=-1, keepdims=True)
    hn = (h * lax.rsqrt(ms + EPS) * g_ref[...]).astype(BF16)
    gate = jax.nn.sigmoid(_dot(hn, wg_ref[...]))
    emb = _dot(p_ref[...].astype(BF16), wp_ref[...])
    o_ref[...] = h + emb * gate


def _ple(h, gain, w_gate, p, w_proj, *, tm=512):
    m, d = h.shape
    e = p.shape[1]
    tm = min(tm, m)
    assert m % tm == 0
    return pl.pallas_call(
        _ple_kernel,
        grid=(m // tm,),
        in_specs=[pl.BlockSpec((tm, d), lambda i: (i, 0)),
                  pl.BlockSpec((1, d), lambda i: (0, 0)),
                  pl.BlockSpec((d, d), lambda i: (0, 0)),
                  pl.BlockSpec((tm, e), lambda i: (i, 0)),
                  pl.BlockSpec((e, d), lambda i: (0, 0))],
        out_specs=pl.BlockSpec((tm, d), lambda i: (i, 0)),
        out_shape=jax.ShapeDtypeStruct((m, d), F32),
        compiler_params=_cparams(("parallel",)),
        name="ple",
    )(h, gain.reshape(1, d), w_gate, p, w_proj)


def _rmsnorm_kernel(x_ref, g_ref, o_ref):
    x = x_ref[...]
    ms = jnp.mean(x * x, axis=-1, keepdims=True)
    o_ref[...] = x * lax.rsqrt(ms + EPS) * g_ref[...]


def _rmsnorm(x, gain, *, tm=1024):
    m, d = x.shape
    tm = min(tm, m)
    assert m % tm == 0
    return pl.pallas_call(
        _rmsnorm_kernel,
        grid=(m // tm,),
        in_specs=[pl.BlockSpec((tm, d), lambda i: (i, 0)),
                  pl.BlockSpec((1, d), lambda i: (0, 0))],
        out_specs=pl.BlockSpec((tm, d), lambda i: (i, 0)),
        out_shape=jax.ShapeDtypeStruct((m, d), F32),
        compiler_params=_cparams(("parallel",)),
        name="final_rmsnorm",
    )(x, gain.reshape(1, d))


def _gdn_gates_kernel(ba_ref, alog_ref, dt_ref, cols_ref, rows_ref, *, n_heads):
    ba = ba_ref[0]
    tt = ba.shape[0]
    beta = jax.nn.sigmoid(ba)
    g = -jnp.exp(alog_ref[...]) * jax.nn.softplus(ba + dt_ref[...])
    row = lax.broadcasted_iota(jnp.int32, (tt, tt), 0)
    col = lax.broadcasted_iota(jnp.int32, (tt, tt), 1)
    tri = jnp.where((row // CHUNK == col // CHUNK) & (col <= row), 1.0, 0.0).astype(F32)
    gc = jnp.dot(tri, g, preferred_element_type=F32, precision=HI)
    lane = lax.broadcasted_iota(jnp.int32, ba.shape, 1)
    cols = jnp.where(lane < n_heads, beta, gc)
    cols_ref[0] = cols
    rows_ref[0] = cols.T


def _gdn_gates(ba, a_log, dt_bias, *, tt=256):
    b, t, _ = ba.shape
    hv = a_log.shape[0]
    assert 2 * hv <= LANES and t % tt == 0 and tt % CHUNK == 0
    pad = lambda v: jnp.zeros((1, LANES), F32).at[0, hv:2 * hv].set(v.astype(F32))
    return pl.pallas_call(
        functools.partial(_gdn_gates_kernel, n_heads=hv),
        grid=(b, t // tt),
        in_specs=[pl.BlockSpec((1, tt, LANES), lambda bi, i: (bi, i, 0)),
                  pl.BlockSpec((1, LANES), lambda bi, i: (0, 0)),
                  pl.BlockSpec((1, LANES), lambda bi, i: (0, 0))],
        out_specs=[pl.BlockSpec((1, tt, LANES), lambda bi, i: (bi, i, 0)),
                   pl.BlockSpec((1, LANES, tt), lambda bi, i: (bi, 0, i))],
        out_shape=[jax.ShapeDtypeStruct((b, t, LANES), F32),
                   jax.ShapeDtypeStruct((b, LANES, t), F32)],
        compiler_params=_cparams(("parallel", "parallel")),
        name="gdn_gates",
    )(ba, pad(a_log), pad(dt_bias))


def _gdn_core_kernel(q_ref, k_ref, v_ref, z_ref, wq_ref, wk_ref, wv_ref, gcol_ref, grow_ref, nw_ref,
                     o_ref, s_ref, qbuf, kbuf, vbuf, *, n_vheads):
    tt = q_ref.shape[1]
    hq = pl.program_id(1)

    @pl.when(pl.program_id(2) == 0)
    def _():
        s_ref[...] = jnp.zeros_like(s_ref)
        qbuf[0:HALO, :] = jnp.zeros((HALO, qbuf.shape[1]), F32)
        kbuf[0:HALO, :] = jnp.zeros((HALO, kbuf.shape[1]), F32)
        vbuf[0:HALO, :] = jnp.zeros((HALO, vbuf.shape[1]), F32)

    def conv_silu(x_ref, w_ref, buf):
        buf[HALO:HALO + tt, :] = x_ref[0].astype(F32)
        w = w_ref[...]
        acc = w[0:1, :] * buf[pl.ds(HALO - (CONV_K - 1), tt), :]
        for j in range(1, CONV_K):
            acc = acc + w[j:j + 1, :] * buf[pl.ds(HALO - (CONV_K - 1) + j, tt), :]
        buf[0:HALO, :] = buf[tt:tt + HALO, :]
        return _silu(acc)

    def l2norm(x):
        return x * lax.rsqrt(jnp.sum(x * x, axis=-1, keepdims=True) + 1e-6)

    q = l2norm(conv_silu(q_ref, wq_ref, qbuf)) * (HEAD ** -0.5)
    k = l2norm(conv_silu(k_ref, wk_ref, kbuf))
    v = conv_silu(v_ref, wv_ref, vbuf)
    z = z_ref[0].astype(F32)
    gcols = gcol_ref[0]
    nw = nw_ref[...]

    ci = lax.broadcasted_iota(jnp.int32, (CHUNK, CHUNK), 0)
    cj = lax.broadcasted_iota(jnp.int32, (CHUNK, CHUNK), 1)
    causal = cj <= ci
    strict = cj < ci
    eye = jnp.where(ci == cj, 1.0, 0.0).astype(F32)
    n_chunks = tt // CHUNK
    units = [(c, e) for c in range(n_chunks) for e in range(2)]

    g_hi = gcols.astype(BF16)
    g_r1 = gcols - g_hi.astype(F32)
    g_mid = g_r1.astype(BF16)
    g_lo = (g_r1 - g_mid.astype(F32)).astype(BF16)
    sel_r = lax.broadcasted_iota(jnp.int32, (3 * LANES, 4 * LANES), 0) % LANES
    sel_c = lax.broadcasted_iota(jnp.int32, (3 * LANES, 4 * LANES), 1) // LANES
    want = jnp.where(sel_c < 2, 2 * hq + sel_c, n_vheads + 2 * hq + sel_c - 2)
    onehot = jnp.where(sel_r == want, 1.0, 0.0).astype(BF16)
    gates_b = _dot(jnp.concatenate([g_hi, g_mid, g_lo], axis=1), onehot)
    gc_rows = [grow_ref[0, pl.ds(n_vheads + 2 * hq + e, 1), :] for e in range(2)]

    q16, k16 = q.astype(BF16), k.astype(BF16)
    qkk = []
    for c in range(n_chunks):
        r = slice(c * CHUNK, (c + 1) * CHUNK)
        qkk.append(_dot_nt(jnp.concatenate([q16[r], k16[r]], axis=0), k16[r]))

    p, tinv, qk_m, rhs16, kdec16, qdec, gl = {}, {}, {}, {}, {}, {}, {}
    for (c, e) in units:
        r = slice(c * CHUNK, (c + 1) * CHUNK)
        bcol = gates_b[r, e * LANES:(e + 1) * LANES]
        gcol = gates_b[r, (2 + e) * LANES:(3 + e) * LANES]
        diff = gcol[:, :CHUNK] - gc_rows[e][:, r]
        decay = jnp.exp(jnp.where(causal, diff, NEG))
        a = jnp.where(strict, qkk[c][CHUNK:] * bcol[:, :CHUNK] * decay, 0.0)
        qk_m[c, e] = (qkk[c][:CHUNK] * decay).astype(BF16)
        p[c, e] = -a
        tinv[c, e] = eye - a
        eg = jnp.exp(gcol)
        kb = k[r] * bcol
        ve = v[r, e * HEAD:(e + 1) * HEAD]
        rhs16[c, e] = jnp.concatenate([kb * eg, ve * bcol], axis=1).astype(BF16)
        glast = gcol[CHUNK - 1:CHUNK, :]
        kdec16[c, e] = (k[r] * jnp.exp(glast - gcol)).astype(BF16)
        qdec[c, e] = q[r] * eg
        gl[c, e] = jnp.exp(glast)
    n_sq = int(math.log2(CHUNK)) - 1
    for un in units:
        p16 = p[un].astype(BF16)
        p[un] = _dot(p16, p16)
    for _ in range(n_sq - 1):
        for un in units:
            p16 = p[un].astype(BF16)
            tp = _dot(jnp.concatenate([tinv[un].astype(BF16), p16], axis=0), p16)
            tinv[un] = tinv[un] + tp[:CHUNK]
            p[un] = tp[CHUNK:]
    for un in units:
        tinv[un] = tinv[un] + _dot(tinv[un].astype(BF16), p[un].astype(BF16))
    wu16 = {}
    for un in units:
        wu16[un] = _dot(tinv[un].astype(BF16), rhs16[un]).astype(BF16)
    lhs16, n_loc, o_loc = {}, {}, {}
    for un in units:
        qk_wu = _dot(qk_m[un], wu16[un])
        kd_wu = _dot_tn(kdec16[un], wu16[un])
        lhs16[un] = jnp.concatenate([kd_wu[:, :HEAD], qdec[un] - qk_wu[:, :HEAD]], axis=0).astype(BF16)
        n_loc[un] = kd_wu[:, HEAD:]
        o_loc[un] = qk_wu[:, HEAD:]

    s = [s_ref[0], s_ref[1]]
    for c in range(n_chunks):
        r = slice(c * CHUNK, (c + 1) * CHUNK)
        for e in range(2):
            rs = _dot(lhs16[c, e], s[e].astype(BF16))
            o = rs[HEAD:] + o_loc[c, e]
            s[e] = s[e] * gl[c, e] - rs[:HEAD] + n_loc[c, e]
            on = o * lax.rsqrt(jnp.mean(o * o, axis=-1, keepdims=True) + EPS) * nw
            ze = z[r, e * HEAD:(e + 1) * HEAD]
            o_ref[0, r, e * HEAD:(e + 1) * HEAD] = (on * _silu(ze)).astype(o_ref.dtype)
    s_ref[0] = s[0]
    s_ref[1] = s[1]


def _gdn_core(proj, conv_w, gcols, grows, norm_w, *, n_qheads, n_vheads, tt=256):
    b, t, _ = proj.shape
    assert n_vheads == 2 * n_qheads and t % tt == 0 and tt % CHUNK == 0
    qk_dim = n_qheads * HEAD
    v_dim = n_vheads * HEAD
    v_blk0 = (2 * qk_dim) // (2 * HEAD)
    z_blk0 = (2 * qk_dim + v_dim) // (2 * HEAD)
    return pl.pallas_call(
        functools.partial(_gdn_core_kernel, n_vheads=n_vheads),
        grid=(b, n_qheads, t // tt),
        in_specs=[pl.BlockSpec((1, tt, HEAD), lambda bi, h, ti: (bi, ti, h)),
                  pl.BlockSpec((1, tt, HEAD), lambda bi, h, ti: (bi, ti, n_qheads + h)),
                  pl.BlockSpec((1, tt, 2 * HEAD), lambda bi, h, ti: (bi, ti, v_blk0 + h)),
                  pl.BlockSpec((1, tt, 2 * HEAD), lambda bi, h, ti: (bi, ti, z_blk0 + h)),
                  pl.BlockSpec((CONV_K, HEAD), lambda bi, h, ti: (0, h)),
                  pl.BlockSpec((CONV_K, HEAD), lambda bi, h, ti: (0, n_qheads + h)),
                  pl.BlockSpec((CONV_K, 2 * HEAD), lambda bi, h, ti: (0, v_blk0 + h)),
                  pl.BlockSpec((1, tt, LANES), lambda bi, h, ti: (bi, ti, 0)),
                  pl.BlockSpec((1, LANES, tt), lambda bi, h, ti: (bi, 0, ti)),
                  pl.BlockSpec((1, HEAD), lambda bi, h, ti: (0, 0))],
        out_specs=pl.BlockSpec((1, tt, 2 * HEAD), lambda bi, h, ti: (bi, ti, h)),
        out_shape=jax.ShapeDtypeStruct((b, t, v_dim), BF16),
        scratch_shapes=[pltpu.VMEM((2, HEAD, HEAD), F32),
                        pltpu.VMEM((tt + HALO, HEAD), F32),
                        pltpu.VMEM((tt + HALO, HEAD), F32),
                        pltpu.VMEM((tt + HALO, 2 * HEAD), F32)],
        compiler_params=_cparams(("parallel", "parallel", "arbitrary")),
        name="gdn_core",
    )(proj, proj, proj, proj, conv_w, conv_w, conv_w, gcols, grows, norm_w.reshape(1, HEAD))


def _diff_attn_kernel(qi_tab, ki_tab, q1_ref, q2_ref, k1_ref, k2_ref, v_ref, lam_ref, sw_ref, o_ref,
                      m_ref, l_ref, acc_ref, *, n_heads, out_scale, lam_init):
    step = pl.program_id(2)
    h = pl.program_id(1)
    qi = qi_tab[step]
    ki = ki_tab[step]
    tq = q1_ref.shape[1]
    tk = k1_ref.shape[1]

    @pl.when(ki == 0)
    def _():
        m_ref[...] = jnp.full_like(m_ref, NEG)
        l_ref[...] = jnp.zeros_like(l_ref)
        acc_ref[...] = jnp.zeros_like(acc_ref)

    slope = jnp.exp2(-8.0 * (h + 1).astype(F32) / n_heads)
    row = lax.broadcasted_iota(jnp.int32, (tq, tk), 0)
    col = lax.broadcasted_iota(jnp.int32, (tq, tk), 1)
    off = ki * tk - qi * tq
    kpos = (lax.broadcasted_iota(jnp.int32, (1, tk), 1) + off).astype(F32)
    bias = slope * kpos
    visible = (col + off) <= row
    v16 = v_ref[0]

    def one_map(c, q_ref, k_ref):
        s = _dot_nt(q_ref[0], k_ref[0]) + bias
        s = jnp.where(visible, s, NEG)
        m_old = m_ref[c]
        m_new = jnp.maximum(m_old, jnp.max(s, axis=-1, keepdims=True))
        p = jnp.exp(s - m_new)
        alpha = jnp.exp(m_old - m_new)
        l_ref[c] = alpha * l_ref[c] + jnp.sum(p, axis=-1, keepdims=True)
        acc_ref[c] = alpha * acc_ref[c] + _dot(p.astype(BF16), v16)
        m_ref[c] = m_new

    one_map(0, q1_ref, k1_ref)
    one_map(1, q2_ref, k2_ref)

    @pl.when((ki + 1) * tk >= (qi + 1) * tq)
    def _():
        lq1, lk1, lq2, lk2 = lam_ref[0:1, :], lam_ref[1:2, :], lam_ref[2:3, :], lam_ref[3:4, :]
        lam = (jnp.exp(jnp.sum(lq1 * lk1, axis=-1, keepdims=True))
               - jnp.exp(jnp.sum(lq2 * lk2, axis=-1, keepdims=True)) + lam_init)
        o = acc_ref[0] / l_ref[0] - lam * (acc_ref[1] / l_ref[1])
        on = o * lax.rsqrt(jnp.mean(o * o, axis=-1, keepdims=True) + EPS) * sw_ref[...]
        o_ref[0] = (on * out_scale).astype(o_ref.dtype)


def _diff_attn(q, kv, lam_vecs, subln_w, *, n_heads, layer_idx, tq=512, tk=512):
    b, t, qw = q.shape
    tq, tk = min(tq, t), min(tk, t)
    assert t % tq == 0 and tq % tk == 0
    dv = 2 * HEAD
    v_blk0 = qw // dv
    pairs = [(i, j) for i in range(t // tq) for j in range(((i + 1) * tq) // tk)]
    qi_tab = jnp.asarray(np.array([p[0] for p in pairs], np.int32))
    ki_tab = jnp.asarray(np.array([p[1] for p in pairs], np.int32))
    lam_init = 0.8 - 0.6 * math.exp(-0.3 * layer_idx)
    grid_spec = pltpu.PrefetchScalarGridSpec(
        num_scalar_prefetch=2,
        grid=(b, n_heads, len(pairs)),
        in_specs=[pl.BlockSpec((1, tq, HEAD), lambda bi, h, s, qt, kt: (bi, qt[s], 2 * h)),
                  pl.BlockSpec((1, tq, HEAD), lambda bi, h, s, qt, kt: (bi, qt[s], 2 * h + 1)),
                  pl.BlockSpec((1, tk, HEAD), lambda bi, h, s, qt, kt: (bi, kt[s], 2 * h)),
                  pl.BlockSpec((1, tk, HEAD), lambda bi, h, s, qt, kt: (bi, kt[s], 2 * h + 1)),
                  pl.BlockSpec((1, tk, dv), lambda bi, h, s, qt, kt: (bi, kt[s], v_blk0 + h)),
                  pl.BlockSpec((4, HEAD), lambda bi, h, s, qt, kt: (0, 0)),
                  pl.BlockSpec((1, dv), lambda bi, h, s, qt, kt: (0, 0))],
        out_specs=pl.BlockSpec((1, tq, dv), lambda bi, h, s, qt, kt: (bi, qt[s], h)),
        scratch_shapes=[pltpu.VMEM((2, tq, 1), F32),
                        pltpu.VMEM((2, tq, 1), F32),
                        pltpu.VMEM((2, tq, dv), F32)])
    return pl.pallas_call(
        functools.partial(_diff_attn_kernel, n_heads=n_heads, out_scale=1.0 - lam_init, lam_init=lam_init),
        grid_spec=grid_spec,
        out_shape=jax.ShapeDtypeStruct((b, t, n_heads * dv), BF16),
        compiler_params=_cparams(("parallel", "parallel", "arbitrary")),
        name="diff_attn",
    )(qi_tab, ki_tab, q, q, kv, kv, kv, lam_vecs, subln_w.reshape(1, dv))


def kernel(x, p, norm_mix, norm_mlp, norm_ple, gdn_w_in, gdn_conv_w, gdn_a_log, gdn_dt_bias, gdn_norm_w, gdn_w_out, kv_norm, w_kv, diff_w_q, diff_lambda_q1, diff_lambda_k1, diff_lambda_q2, diff_lambda_k2, diff_subln_w, diff_w_o, mlp_w1, mlp_w2, ple_w_proj, ple_w_gate, final_norm):
    b, t, d = x.shape
    m = b * t
    depth = norm_mix.shape[0]
    n_a = gdn_w_in.shape[0]
    n_vheads = gdn_a_log.shape[1]
    v_dim = n_vheads * HEAD
    conv_dim = gdn_conv_w.shape[2]
    qk_dim = (conv_dim - v_dim) // 2
    n_qheads = qk_dim // HEAD
    main_w = conv_dim + v_dim
    n_dheads = diff_w_q.shape[2] // (2 * HEAD)

    h = x.reshape(m, d)
    kv = None
    for i in range(depth):
        if i < n_a:
            w_in = gdn_w_in[i]
            w_main = w_in[:, :main_w].astype(BF16)
            w_ba = jnp.pad(w_in[:, main_w:], ((0, 0), (0, LANES - 2 * n_vheads))).astype(BF16)
            proj = _norm_matmul(h, norm_mix[i], w_main)
            ba = _norm_matmul(h, norm_mix[i], w_ba, out_dtype=F32)
            gcols, grows = _gdn_gates(ba.reshape(b, t, LANES), gdn_a_log[i], gdn_dt_bias[i])
            o = _gdn_core(proj.reshape(b, t, main_w), gdn_conv_w[i], gcols, grows,
                          gdn_norm_w[i], n_qheads=n_qheads, n_vheads=n_vheads)
            h = _matmul_residual(o.reshape(m, v_dim), gdn_w_out[i].astype(BF16), h)
        else:
            j = i - n_a
            q = _norm_matmul(h, norm_mix[i], diff_w_q[j].astype(BF16), scale=HEAD ** -0.5)
            lam_vecs = jnp.stack([diff_lambda_q1[j], diff_lambda_k1[j],
                                  diff_lambda_q2[j], diff_lambda_k2[j]]).astype(F32)
            o = _diff_attn(q.reshape(b, t, -1), kv, lam_vecs, diff_subln_w[j],
                           n_heads=n_dheads, layer_idx=i)
            h = _matmul_residual(o.reshape(m, -1), diff_w_o[j].astype(BF16), h)
        a = _norm_matmul(h, norm_mlp[i], mlp_w1[i].astype(BF16), act="relu2")
        h = _matmul_residual(a, mlp_w2[i].astype(BF16), h)
        h = _ple(h, norm_ple[i], ple_w_gate[i].astype(BF16), p[i].reshape(m, -1),
                 ple_w_proj[i].astype(BF16))
        if i == n_a - 1:
            kv = _norm_matmul(h, kv_norm, w_kv.astype(BF16)).reshape(b, t, -1)
    return _rmsnorm(h, final_norm).reshape(b, t, d)
```

```python
import functools
import math

import numpy as np
import jax
import jax.numpy as jnp
from jax import lax
from jax.experimental import pallas as pl
from jax.experimental.pallas import tpu as pltpu

F32 = jnp.float32
BF16 = jnp.bfloat16
EPS = 1e-6
LANES = 128
HEAD = 128
CHUNK = 64
CONV_K = 4
HALO = 8
NEG = -1e30
LOG2E = 1.4426950408889634
VMEM_LIMIT = 56 * 1024 * 1024
HI = lax.Precision.HIGHEST


def _cparams(sem):
    return pltpu.CompilerParams(dimension_semantics=sem, vmem_limit_bytes=VMEM_LIMIT)


def _tile(n, target):
    if n <= target:
        return n
    t = target - target % LANES
    while n % t:
        t -= LANES
    assert t > 0
    return t


def _dot(a, b):
    return jnp.dot(a, b, preferred_element_type=F32)


def _dot_nt(a, b):
    return lax.dot_general(a, b, (((1,), (1,)), ((), ())), preferred_element_type=F32)


def _dot_tn(a, b):
    return lax.dot_general(a, b, (((0,), (0,)), ((), ())), preferred_element_type=F32)


def _silu(x):
    hx = 0.5 * x
    return hx + hx * jnp.tanh(hx)


def _norm_matmul_kernel(x_ref, g_ref, w_ref, o_ref, xn_ref, *, act, scale):
    @pl.when(pl.program_id(1) == 0)
    def _():
        x = x_ref[...]
        ms = jnp.mean(x * x, axis=-1, keepdims=True)
        xn_ref[...] = (x * lax.rsqrt(ms + EPS) * g_ref[...]).astype(BF16)

    acc = _dot(xn_ref[...], w_ref[...])
    if act == "relu2":
        acc = jnp.square(jnp.maximum(acc, 0.0))
    if scale != 1.0:
        acc = acc * scale
    o_ref[...] = acc.astype(o_ref.dtype)


def _norm_matmul(x, gain, w, *, act=None, scale=1.0, out_dtype=BF16, tm=1024, tn=1024):
    m, d = x.shape
    n = w.shape[1]
    tm, tn = _tile(m, tm), _tile(n, tn)
    return pl.pallas_call(
        functools.partial(_norm_matmul_kernel, act=act, scale=scale),
        grid=(m // tm, n // tn),
        in_specs=[pl.BlockSpec((tm, d), lambda i, j: (i, 0)),
                  pl.BlockSpec((1, d), lambda i, j: (0, 0)),
                  pl.BlockSpec((d, tn), lambda i, j: (0, j))],
        out_specs=pl.BlockSpec((tm, tn), lambda i, j: (i, j)),
        out_shape=jax.ShapeDtypeStruct((m, n), out_dtype),
        scratch_shapes=[pltpu.VMEM((tm, d), BF16)],
        compiler_params=_cparams(("parallel", "arbitrary")),
        name="norm_matmul",
    )(x, gain.reshape(1, d), w)


def _matmul_residual_kernel(a_ref, w_ref, h_ref, o_ref):
    @pl.when(pl.program_id(2) == 0)
    def _():
        o_ref[...] = h_ref[...]

    o_ref[...] += _dot(a_ref[...], w_ref[...])


def _matmul_residual(a, w, h, *, tm=1024, tn=1024, tk=2048):
    m, k = a.shape
    n = w.shape[1]
    tm, tn, tk = _tile(m, tm), _tile(n, tn), _tile(k, tk)
    return pl.pallas_call(
        _matmul_residual_kernel,
        grid=(m // tm, n // tn, k // tk),
        in_specs=[pl.BlockSpec((tm, tk), lambda i, j, kk: (i, kk)),
                  pl.BlockSpec((tk, tn), lambda i, j, kk: (kk, j)),
                  pl.BlockSpec((tm, tn), lambda i, j, kk: (i, j))],
        out_specs=pl.BlockSpec((tm, tn), lambda i, j, kk: (i, j)),
        out_shape=jax.ShapeDtypeStruct((m, n), F32),
        compiler_params=_cparams(("parallel", "parallel", "arbitrary")),
        name="matmul_residual",
    )(a, w, h)


def _ple_kernel(h_ref, g_ref, wg_ref, p_ref, wp_ref, o_ref):
    h = h_ref[...]
    ms = jnp.mean(h * h, axis=-1, keepdims=True)
    hn = (h * lax.rsqrt(ms + EPS) * g_ref[...]).astype(BF16)
    gate = jax.nn.sigmoid(_dot(hn, wg_ref[...]))
    emb = _dot(p_ref[...].astype(BF16), wp_ref[...])
    o_ref[...] = h + emb * gate


def _ple(h, gain, w_gate, p, w_proj, *, tm=512):
    m, d = h.shape
    e = p.shape[1]
    tm = min(tm, m)
    assert m % tm == 0
    return pl.pallas_call(
        _ple_kernel,
        grid=(m // tm,),
        in_specs=[pl.BlockSpec((tm, d), lambda i: (i, 0)),
                  pl.BlockSpec((1, d), lambda i: (0, 0)),
                  pl.BlockSpec((d, d), lambda i: (0, 0)),
                  pl.BlockSpec((tm, e), lambda i: (i, 0)),
                  pl.BlockSpec((e, d), lambda i: (0, 0))],
        out_specs=pl.BlockSpec((tm, d), lambda i: (i, 0)),
        out_shape=jax.ShapeDtypeStruct((m, d), F32),
        compiler_params=_cparams(("parallel",)),
        name="ple",
    )(h, gain.reshape(1, d), w_gate, p, w_proj)


def _rmsnorm_kernel(x_ref, g_ref, o_ref):
    x = x_ref[...]
    ms = jnp.mean(x * x, axis=-1, keepdims=True)
    o_ref[...] = x * lax.rsqrt(ms + EPS) * g_ref[...]


def _rmsnorm(x, gain, *, tm=1024):
    m, d = x.shape
    tm = min(tm, m)
    assert m % tm == 0
    return pl.pallas_call(
        _rmsnorm_kernel,
        grid=(m // tm,),
        in_specs=[pl.BlockSpec((tm, d), lambda i: (i, 0)),
                  pl.BlockSpec((1, d), lambda i: (0, 0))],
        out_specs=pl.BlockSpec((tm, d), lambda i: (i, 0)),
        out_shape=jax.ShapeDtypeStruct((m, d), F32),
        compiler_params=_cparams(("parallel",)),
        name="final_rmsnorm",
    )(x, gain.reshape(1, d))


def _gdn_gates_kernel(ba_ref, alog_ref, dt_ref, cols_ref, rows_ref, *, n_heads):
    ba = ba_ref[0]
    tt = ba.shape[0]
    beta = jax.nn.sigmoid(ba)
    g = -jnp.exp(alog_ref[...]) * jax.nn.softplus(ba + dt_ref[...])
    row = lax.broadcasted_iota(jnp.int32, (tt, tt), 0)
    col = lax.broadcasted_iota(jnp.int32, (tt, tt), 1)
    tri = jnp.where((row // CHUNK == col // CHUNK) & (col <= row), 1.0, 0.0).astype(F32)
    gc = jnp.dot(tri, g, preferred_element_type=F32, precision=HI)
    lane = lax.broadcasted_iota(jnp.int32, ba.shape, 1)
    cols = jnp.where(lane < n_heads, beta, gc)
    cols_ref[0] = cols
    rows_ref[0] = cols.T


def _gdn_gates(ba, a_log, dt_bias, *, tt=256):
    b, t, _ = ba.shape
    hv = a_log.shape[0]
    assert 2 * hv <= LANES and t % tt == 0 and tt % CHUNK == 0
    pad = lambda v: jnp.zeros((1, LANES), F32).at[0, hv:2 * hv].set(v.astype(F32))
    return pl.pallas_call(
        functools.partial(_gdn_gates_kernel, n_heads=hv),
        grid=(b, t // tt),
        in_specs=[pl.BlockSpec((1, tt, LANES), lambda bi, i: (bi, i, 0)),
                  pl.BlockSpec((1, LANES), lambda bi, i: (0, 0)),
                  pl.BlockSpec((1, LANES), lambda bi, i: (0, 0))],
        out_specs=[pl.BlockSpec((1, tt, LANES), lambda bi, i: (bi, i, 0)),
                   pl.BlockSpec((1, LANES, tt), lambda bi, i: (bi, 0, i))],
        out_shape=[jax.ShapeDtypeStruct((b, t, LANES), F32),
                   jax.ShapeDtypeStruct((b, LANES, t), F32)],
        compiler_params=_cparams(("parallel", "parallel")),
        name="gdn_gates",
    )(ba, pad(a_log), pad(dt_bias))


def _gdn_core_kernel(q_ref, k_ref, v_ref, z_ref, wq_ref, wk_ref, wv_ref, gcol_ref, grow_ref, nw_ref, sel_ref,
                     o_ref, s_ref, qbuf, kbuf, vbuf, *, n_vheads):
    tt = q_ref.shape[1]
    n_g = q_ref.shape[2] // HEAD
    n_e = 2 * n_g
    hv0 = n_e * pl.program_id(1)

    @pl.when(pl.program_id(2) == 0)
    def _():
        s_ref[...] = jnp.zeros_like(s_ref)
        qbuf[...] = jnp.zeros_like(qbuf)
        kbuf[...] = jnp.zeros_like(kbuf)
        vbuf[...] = jnp.zeros_like(vbuf)

    def conv_silu(x_ref, w_ref, buf):
        xe = jnp.concatenate([buf[...], x_ref[0].astype(F32)], axis=0)
        w = w_ref[...]
        acc = w[0:1, :] * xe
        for j in range(1, CONV_K):
            acc = pltpu.roll(acc, 1, 0) + w[j:j + 1, :] * xe
        buf[...] = xe[tt:tt + HALO, :]
        return _silu(acc[HALO:, :])

    def l2norm_heads(x, scale):
        out = []
        for g in range(n_g):
            xg = x[:, g * HEAD:(g + 1) * HEAD]
            out.append(xg * (lax.rsqrt(jnp.sum(xg * xg, axis=-1, keepdims=True) + 1e-6) * scale))
        return out

    q = l2norm_heads(conv_silu(q_ref, wq_ref, qbuf), HEAD ** -0.5)
    k = l2norm_heads(conv_silu(k_ref, wk_ref, kbuf), 1.0)
    v = conv_silu(v_ref, wv_ref, vbuf)
    gcols = gcol_ref[0]
    nw = nw_ref[...]

    ci = lax.broadcasted_iota(jnp.int32, (CHUNK, CHUNK), 0)
    cj = lax.broadcasted_iota(jnp.int32, (CHUNK, CHUNK), 1)
    causal = cj <= ci
    strict = cj < ci
    eye = jnp.where(ci == cj, 1.0, 0.0).astype(F32)
    n_chunks = tt // CHUNK
    units = [(c, e) for c in range(n_chunks) for e in range(n_e)]

    g_hi = gcols.astype(BF16)
    g_r1 = gcols - g_hi.astype(F32)
    g_mid = g_r1.astype(BF16)
    g_lo = (g_r1 - g_mid.astype(F32)).astype(BF16)
    gates_b = _dot(jnp.concatenate([g_hi, g_mid, g_lo], axis=1), sel_ref[0])
    gc_rows = [grow_ref[0, pl.ds(n_vheads + hv0 + e, 1), :] for e in range(n_e)]

    q16 = [x.astype(BF16) for x in q]
    k16 = [x.astype(BF16) for x in k]
    qkk = {}
    for c in range(n_chunks):
        r = slice(c * CHUNK, (c + 1) * CHUNK)
        for g in range(n_g):
            qkk[c, g] = _dot_nt(jnp.concatenate([q16[g][r], k16[g][r]], axis=0), k16[g][r])

    p, tinv, qk_m, rhs16, kdec16, qdec, gl = {}, {}, {}, {}, {}, {}, {}
    for (c, e) in units:
        r = slice(c * CHUNK, (c + 1) * CHUNK)
        g = e // 2
        bcol = gates_b[r, e * LANES:(e + 1) * LANES]
        gcol = gates_b[r, (n_e + e) * LANES:(n_e + e + 1) * LANES]
        diff = gcol[:, :CHUNK] - gc_rows[e][:, r]
        decay = jnp.exp(jnp.where(causal, diff, NEG))
        a = jnp.where(strict, qkk[c, g][CHUNK:] * bcol[:, :CHUNK] * decay, 0.0)
        qk_m[c, e] = (qkk[c, g][:CHUNK] * decay).astype(BF16)
        p[c, e] = -a
        tinv[c, e] = eye - a
        eg = jnp.exp(gcol)
        kb = k[g][r] * bcol
        ve = v[r, e * HEAD:(e + 1) * HEAD]
        rhs16[c, e] = jnp.concatenate([kb * eg, ve * bcol], axis=1).astype(BF16)
        glast = gcol[CHUNK - 1:CHUNK, :]
        kdec16[c, e] = (k[g][r] * jnp.exp(glast - gcol)).astype(BF16)
        qdec[c, e] = q[g][r] * eg
        gl[c, e] = jnp.exp(glast)
    n_sq = int(math.log2(CHUNK)) - 1
    for un in units:
        p16 = p[un].astype(BF16)
        p[un] = _dot(p16, p16)
    for _ in range(n_sq - 1):
        for un in units:
            p16 = p[un].astype(BF16)
            tp = _dot(jnp.concatenate([tinv[un].astype(BF16), p16], axis=0), p16)
            tinv[un] = tinv[un] + tp[:CHUNK]
            p[un] = tp[CHUNK:]
    for un in units:
        tinv[un] = tinv[un] + _dot(tinv[un].astype(BF16), p[un].astype(BF16))
    wu16 = {}
    for un in units:
        wu16[un] = _dot(tinv[un].astype(BF16), rhs16[un]).astype(BF16)
    lhs16, n_loc, o_loc = {}, {}, {}
    for un in units:
        qk_wu = _dot(qk_m[un], wu16[un])
        kd_wu = _dot_tn(kdec16[un], wu16[un])
        lhs16[un] = jnp.concatenate([kd_wu[:, :HEAD], qdec[un] - qk_wu[:, :HEAD]], axis=0).astype(BF16)
        n_loc[un] = kd_wu[:, HEAD:]
        o_loc[un] = qk_wu[:, HEAD:]

    s = [s_ref[e] for e in range(n_e)]
    for c in range(n_chunks):
        r = slice(c * CHUNK, (c + 1) * CHUNK)
        rs = [_dot(lhs16[c, e], s[e].astype(BF16)) for e in range(n_e)]
        for e in range(n_e):
            o = rs[e][HEAD:] + o_loc[c, e]
            s[e] = s[e] * gl[c, e] - rs[e][:HEAD] + n_loc[c, e]
            on = o * lax.rsqrt(jnp.mean(o * o, axis=-1, keepdims=True) + EPS) * nw
            ze = z_ref[0, r, e * HEAD:(e + 1) * HEAD].astype(F32)
            o_ref[0, r, e * HEAD:(e + 1) * HEAD] = (on * _silu(ze)).astype(o_ref.dtype)
    for e in range(n_e):
        s_ref[e] = s[e]


def _gdn_core(proj, conv_w, gcols, grows, norm_w, *, n_qheads, n_vheads, tt=256, group=4):
    b, t, _ = proj.shape
    group = min(group, n_qheads)
    assert n_vheads == 2 * n_qheads and n_qheads % group == 0 and t % tt == 0 and tt % CHUNK == 0
    qk_dim = n_qheads * HEAD
    v_dim = n_vheads * HEAD
    qw, vw = group * HEAD, 2 * group * HEAD
    n_groups = n_qheads // group
    assert (2 * qk_dim) % vw == 0 and (2 * qk_dim + v_dim) % vw == 0
    v_blk0 = (2 * qk_dim) // vw
    z_blk0 = (2 * qk_dim + v_dim) // vw
    n_e = 2 * group
    sel = np.zeros((n_groups, 3 * LANES, 2 * n_e * LANES), np.float32)
    for gi in range(n_groups):
        for e in range(n_e):
            for piece in range(3):
                sel[gi, piece * LANES + gi * n_e + e, e * LANES:(e + 1) * LANES] = 1.0
                sel[gi, piece * LANES + n_vheads + gi * n_e + e, (n_e + e) * LANES:(n_e + e + 1) * LANES] = 1.0
    return pl.pallas_call(
        functools.partial(_gdn_core_kernel, n_vheads=n_vheads),
        grid=(b, n_groups, t // tt),
        in_specs=[pl.BlockSpec((1, tt, qw), lambda bi, h, ti: (bi, ti, h)),
                  pl.BlockSpec((1, tt, qw), lambda bi, h, ti: (bi, ti, n_groups + h)),
                  pl.BlockSpec((1, tt, vw), lambda bi, h, ti: (bi, ti, v_blk0 + h)),
                  pl.BlockSpec((1, tt, vw), lambda bi, h, ti: (bi, ti, z_blk0 + h)),
                  pl.BlockSpec((CONV_K, qw), lambda bi, h, ti: (0, h)),
                  pl.BlockSpec((CONV_K, qw), lambda bi, h, ti: (0, n_groups + h)),
                  pl.BlockSpec((CONV_K, vw), lambda bi, h, ti: (0, v_blk0 + h)),
                  pl.BlockSpec((1, tt, LANES), lambda bi, h, ti: (bi, ti, 0)),
                  pl.BlockSpec((1, LANES, tt), lambda bi, h, ti: (bi, 0, ti)),
                  pl.BlockSpec((1, HEAD), lambda bi, h, ti: (0, 0)),
                  pl.BlockSpec((1, 3 * LANES, 2 * n_e * LANES), lambda bi, h, ti: (h, 0, 0))],
        out_specs=pl.BlockSpec((1, tt, vw), lambda bi, h, ti: (bi, ti, h)),
        out_shape=jax.ShapeDtypeStruct((b, t, v_dim), BF16),
        scratch_shapes=[pltpu.VMEM((2 * group, HEAD, HEAD), F32),
                        pltpu.VMEM((HALO, qw), F32),
                        pltpu.VMEM((HALO, qw), F32),
                        pltpu.VMEM((HALO, vw), F32)],
        compiler_params=_cparams(("parallel", "parallel", "arbitrary")),
        name="gdn_core",
    )(proj, proj, proj, proj, conv_w, conv_w, conv_w, gcols, grows, norm_w.reshape(1, HEAD), jnp.asarray(sel, BF16))


def _diff_attn_kernel(qi_tab, ki_tab, q1_ref, q2_ref, k1_ref, k2_ref, v_ref, lam_ref, sw_ref, o_ref,
                      m_ref, l_ref, acc_ref, *, n_heads, out_scale, lam_init):
    step = pl.program_id(2)
    h = pl.program_id(1)
    qi = qi_tab[step]
    ki = ki_tab[step]
    tq = q1_ref.shape[1]
    tk = k1_ref.shape[1]

    @pl.when(ki == 0)
    def _():
        m_ref[...] = jnp.full_like(m_ref, NEG)
        l_ref[...] = jnp.zeros_like(l_ref)
        acc_ref[...] = jnp.zeros_like(acc_ref)

    slope = jnp.exp2(-8.0 * (h + 1).astype(F32) / n_heads) * LOG2E
    off = ki * tk - qi * tq
    kpos = (lax.broadcasted_iota(jnp.int32, (1, tk), 1) + off).astype(F32)
    bias = slope * kpos

    def update(masked):
        v16 = v_ref[0]
        scores = [_dot_nt(q1_ref[0], k1_ref[0]) + bias, _dot_nt(q2_ref[0], k2_ref[0]) + bias]
        if masked:
            row = lax.broadcasted_iota(jnp.int32, (tq, tk), 0)
            col = lax.broadcasted_iota(jnp.int32, (tq, tk), 1)
            visible = (col + off) <= row
            scores = [jnp.where(visible, s, NEG) for s in scores]
        for c, s in enumerate(scores):
            m_old = m_ref[c]
            m_new = jnp.maximum(m_old, jnp.max(s, axis=-1, keepdims=True))
            p = jnp.exp2(s - jnp.concatenate([m_new] * (tk // LANES), axis=1))
            alpha = jnp.exp2(m_old - m_new)
            l_ref[c] = alpha * l_ref[c] + jnp.sum(p, axis=-1, keepdims=True)
            acc_ref[c] = (jnp.concatenate([alpha] * (acc_ref.shape[2] // LANES), axis=1) * acc_ref[c]
                          + _dot(p.astype(BF16), v16))
            m_ref[c] = m_new

    touches_diag = (ki + 1) * tk > qi * tq + 1
    pl.when(touches_diag)(functools.partial(update, True))
    pl.when(jnp.logical_not(touches_diag))(functools.partial(update, False))

    @pl.when((ki + 1) * tk >= (qi + 1) * tq)
    def _():
        lq1, lk1, lq2, lk2 = lam_ref[0:1, :], lam_ref[1:2, :], lam_ref[2:3, :], lam_ref[3:4, :]
        lam = (jnp.exp(jnp.sum(lq1 * lk1, axis=-1, keepdims=True))
               - jnp.exp(jnp.sum(lq2 * lk2, axis=-1, keepdims=True)) + lam_init)
        rep = acc_ref.shape[2] // LANES
        inv1 = jnp.concatenate([1.0 / l_ref[0]] * rep, axis=1)
        inv2 = jnp.concatenate([1.0 / l_ref[1]] * rep, axis=1)
        o = acc_ref[0] * inv1 - lam * (acc_ref[1] * inv2)
        on = o * lax.rsqrt(jnp.mean(o * o, axis=-1, keepdims=True) + EPS) * sw_ref[...]
        o_ref[0] = (on * out_scale).astype(o_ref.dtype)


def _diff_attn(q, kv, lam_vecs, subln_w, *, n_heads, layer_idx, tq=512, tk=512):
    b, t, qw = q.shape
    tq, tk = min(tq, t), min(tk, t)
    assert t % tq == 0 and tq % tk == 0
    dv = 2 * HEAD
    v_blk0 = qw // dv
    pairs = [(i, j) for i in range(t // tq) for j in range(((i + 1) * tq) // tk)]
    qi_tab = jnp.asarray(np.array([p[0] for p in pairs], np.int32))
    ki_tab = jnp.asarray(np.array([p[1] for p in pairs], np.int32))
    lam_init = 0.8 - 0.6 * math.exp(-0.3 * layer_idx)
    grid_spec = pltpu.PrefetchScalarGridSpec(
        num_scalar_prefetch=2,
        grid=(b, n_heads, len(pairs)),
        in_specs=[pl.BlockSpec((1, tq, HEAD), lambda bi, h, s, qt, kt: (bi, qt[s], 2 * h)),
                  pl.BlockSpec((1, tq, HEAD), lambda bi, h, s, qt, kt: (bi, qt[s], 2 * h + 1)),
                  pl.BlockSpec((1, tk, HEAD), lambda bi, h, s, qt, kt: (bi, kt[s], 2 * h)),
                  pl.BlockSpec((1, tk, HEAD), lambda bi, h, s, qt, kt: (bi, kt[s], 2 * h + 1)),
                  pl.BlockSpec((1, tk, dv), lambda bi, h, s, qt, kt: (bi, kt[s], v_blk0 + h)),
                  pl.BlockSpec((4, HEAD), lambda bi, h, s, qt, kt: (0, 0)),
                  pl.BlockSpec((1, dv), lambda bi, h, s, qt, kt: (0, 0))],
        out_specs=pl.BlockSpec((1, tq, dv), lambda bi, h, s, qt, kt: (bi, qt[s], h)),
        scratch_shapes=[pltpu.VMEM((2, tq, LANES), F32),
                        pltpu.VMEM((2, tq, LANES), F32),
                        pltpu.VMEM((2, tq, dv), F32)])
    return pl.pallas_call(
        functools.partial(_diff_attn_kernel, n_heads=n_heads, out_scale=1.0 - lam_init, lam_init=lam_init),
        grid_spec=grid_spec,
        out_shape=jax.ShapeDtypeStruct((b, t, n_heads * dv), BF16),
        compiler_params=_cparams(("parallel", "parallel", "arbitrary")),
        name="diff_attn",
    )(qi_tab, ki_tab, q, q, kv, kv, kv, lam_vecs, subln_w.reshape(1, dv))


def kernel(x, p, norm_mix, norm_mlp, norm_ple, gdn_w_in, gdn_conv_w, gdn_a_log, gdn_dt_bias, gdn_norm_w, gdn_w_out, kv_norm, w_kv, diff_w_q, diff_lambda_q1, diff_lambda_k1, diff_lambda_q2, diff_lambda_k2, diff_subln_w, diff_w_o, mlp_w1, mlp_w2, ple_w_proj, ple_w_gate, final_norm):
    b, t, d = x.shape
    m = b * t
    depth = norm_mix.shape[0]
    n_a = gdn_w_in.shape[0]
    n_vheads = gdn_a_log.shape[1]
    v_dim = n_vheads * HEAD
    conv_dim = gdn_conv_w.shape[2]
    qk_dim = (conv_dim - v_dim) // 2
    n_qheads = qk_dim // HEAD
    main_w = conv_dim + v_dim
    n_dheads = diff_w_q.shape[2] // (2 * HEAD)

    h = x.reshape(m, d)
    kv = None
    for i in range(depth):
        if i < n_a:
            w_in = gdn_w_in[i]
            w_main = w_in[:, :main_w].astype(BF16)
            w_ba = jnp.pad(w_in[:, main_w:], ((0, 0), (0, LANES - 2 * n_vheads))).astype(BF16)
            proj = _norm_matmul(h, norm_mix[i], w_main)
            ba = _norm_matmul(h, norm_mix[i], w_ba, out_dtype=F32)
            gcols, grows = _gdn_gates(ba.reshape(b, t, LANES), gdn_a_log[i], gdn_dt_bias[i])
            o = _gdn_core(proj.reshape(b, t, main_w), gdn_conv_w[i], gcols, grows,
                          gdn_norm_w[i], n_qheads=n_qheads, n_vheads=n_vheads)
            h = _matmul_residual(o.reshape(m, v_dim), gdn_w_out[i].astype(BF16), h)
        else:
            j = i - n_a
            q = _norm_matmul(h, norm_mix[i], diff_w_q[j].astype(BF16), scale=HEAD ** -0.5 * LOG2E)
            lam_vecs = jnp.stack([diff_lambda_q1[j], diff_lambda_k1[j],
                                  diff_lambda_q2[j], diff_lambda_k2[j]]).astype(F32)
            o = _diff_attn(q.reshape(b, t, -1), kv, lam_vecs, diff_subln_w[j],
                           n_heads=n_dheads, layer_idx=i)
            h = _matmul_residual(o.reshape(m, -1), diff_w_o[j].astype(BF16), h)
        a = _norm_matmul(h, norm_mlp[i], mlp_w1[i].astype(BF16), act="relu2")
        h = _matmul_residual(a, mlp_w2[i].astype(BF16), h)
        h = _ple(h, norm_ple[i], ple_w_gate[i].astype(BF16), p[i].reshape(m, -1),
                 ple_w_proj[i].astype(BF16))
        if i == n_a - 1:
            kv = _norm_matmul(h, kv_norm, w_kv.astype(BF16)).reshape(b, t, -1)
    return _rmsnorm(h, final_norm).reshape(b, t, d)
```

```python
import functools
import math

import numpy as np
import jax
import jax.numpy as jnp
from jax import lax
from jax.experimental import pallas as pl
from jax.experimental.pallas import tpu as pltpu

F32 = jnp.float32
BF16 = jnp.bfloat16
EPS = 1e-6
LANES = 128
HEAD = 128
CHUNK = 64
SUB = 16
CONV_K = 4
HALO = 8
NEG = -1e30
LOG2E = 1.4426950408889634
VMEM_LIMIT = 56 * 1024 * 1024
HI = lax.Precision.HIGHEST


def _cparams(sem):
    return pltpu.CompilerParams(dimension_semantics=sem, vmem_limit_bytes=VMEM_LIMIT)


def _tile(n, target):
    if n <= target:
        return n
    t = target - target % LANES
    while n % t:
        t -= LANES
    assert t > 0
    return t


def _dot(a, b):
    return jnp.dot(a, b, preferred_element_type=F32)


def _dot_nt(a, b):
    return lax.dot_general(a, b, (((1,), (1,)), ((), ())), preferred_element_type=F32)


def _dot_tn(a, b):
    return lax.dot_general(a, b, (((0,), (0,)), ((), ())), preferred_element_type=F32)


def _silu(x):
    hx = 0.5 * x
    return hx + hx * jnp.tanh(hx)


def _norm_matmul_kernel(x_ref, g_ref, w_ref, o_ref, xn_ref, *, act, scale):
    @pl.when(pl.program_id(1) == 0)
    def _():
        x = x_ref[...]
        ms = jnp.mean(x * x, axis=-1, keepdims=True)
        xn_ref[...] = (x * lax.rsqrt(ms + EPS) * g_ref[...]).astype(BF16)

    acc = _dot(xn_ref[...], w_ref[...].astype(BF16))
    if act == "relu2":
        acc = jnp.square(jnp.maximum(acc, 0.0))
    if scale != 1.0:
        acc = acc * scale
    o_ref[...] = acc.astype(o_ref.dtype)


def _norm_matmul(x, gain, w, layer, *, n=None, col0=0, act=None, scale=1.0, out_dtype=BF16, tm=1024, tn=1024):
    m, d = x.shape
    n = w.shape[2] if n is None else n
    tm, tn = _tile(m, tm), _tile(n, tn)
    return pl.pallas_call(
        functools.partial(_norm_matmul_kernel, act=act, scale=scale),
        grid=(m // tm, n // tn),
        in_specs=[pl.BlockSpec((tm, d), lambda i, j: (i, 0)),
                  pl.BlockSpec((1, d), lambda i, j: (0, 0)),
                  pl.BlockSpec((None, d, tn), lambda i, j: (layer, 0, col0 + j))],
        out_specs=pl.BlockSpec((tm, tn), lambda i, j: (i, j)),
        out_shape=jax.ShapeDtypeStruct((m, n), out_dtype),
        scratch_shapes=[pltpu.VMEM((tm, d), BF16)],
        compiler_params=_cparams(("parallel", "arbitrary")),
        name="norm_matmul",
    )(x, gain.reshape(1, d), w)


def _matmul_residual_kernel(a_ref, w_ref, h_ref, o_ref):
    @pl.when(pl.program_id(2) == 0)
    def _():
        o_ref[...] = h_ref[...]

    o_ref[...] += _dot(a_ref[...], w_ref[...].astype(BF16))


def _matmul_residual(a, w, layer, h, *, tm=1024, tn=1024, tk=2048):
    m, k = a.shape
    n = w.shape[2]
    tm, tn, tk = _tile(m, tm), _tile(n, tn), _tile(k, tk)
    return pl.pallas_call(
        _matmul_residual_kernel,
        grid=(m // tm, n // tn, k // tk),
        in_specs=[pl.BlockSpec((tm, tk), lambda i, j, kk: (i, kk)),
                  pl.BlockSpec((None, tk, tn), lambda i, j, kk: (layer, kk, j)),
                  pl.BlockSpec((tm, tn), lambda i, j, kk: (i, j))],
        out_specs=pl.BlockSpec((tm, tn), lambda i, j, kk: (i, j)),
        out_shape=jax.ShapeDtypeStruct((m, n), F32),
        compiler_params=_cparams(("parallel", "parallel", "arbitrary")),
        name="matmul_residual",
    )(a, w, h)


def _ple_kernel(h_ref, g_ref, wg_ref, p_ref, wp_ref, fg_ref, o_ref, wg16, wp16, *, final_norm):
    @pl.when(pl.program_id(0) == 0)
    def _():
        wg16[...] = wg_ref[...].astype(BF16)
        wp16[...] = wp_ref[...].astype(BF16)

    h = h_ref[...]
    ms = jnp.mean(h * h, axis=-1, keepdims=True)
    hn = (h * lax.rsqrt(ms + EPS) * g_ref[...]).astype(BF16)
    gate = jax.nn.sigmoid(_dot(hn, wg16[...]))
    emb = _dot(p_ref[...].astype(BF16), wp16[...])
    out = h + emb * gate
    if final_norm:
        out = out * lax.rsqrt(jnp.mean(out * out, axis=-1, keepdims=True) + EPS) * fg_ref[...]
    o_ref[...] = out


def _ple(h, gain, w_gate, p, w_proj, layer, final_gain, *, final_norm, tm=512):
    m, d = h.shape
    e = p.shape[2]
    tm = min(tm, m)
    assert m % tm == 0
    once = pl.Buffered(1)
    return pl.pallas_call(
        functools.partial(_ple_kernel, final_norm=final_norm),
        grid=(m // tm,),
        in_specs=[pl.BlockSpec((tm, d), lambda i: (i, 0)),
                  pl.BlockSpec((1, d), lambda i: (0, 0)),
                  pl.BlockSpec((None, d, d), lambda i: (layer, 0, 0), pipeline_mode=once),
                  pl.BlockSpec((None, tm, e), lambda i: (layer, i, 0)),
                  pl.BlockSpec((None, e, d), lambda i: (layer, 0, 0), pipeline_mode=once),
                  pl.BlockSpec((1, d), lambda i: (0, 0))],
        out_specs=pl.BlockSpec((tm, d), lambda i: (i, 0)),
        out_shape=jax.ShapeDtypeStruct((m, d), F32),
        scratch_shapes=[pltpu.VMEM((d, d), BF16), pltpu.VMEM((e, d), BF16)],
        compiler_params=_cparams(("arbitrary",)),
        name="ple",
    )(h, gain.reshape(1, d), w_gate, p, w_proj, final_gain.reshape(1, d))


def _gdn_gates_kernel(ba_ref, alog_ref, dt_ref, cols_ref, rows_ref, *, n_heads):
    lane = lax.broadcasted_iota(jnp.int32, ba_ref.shape[1:], 1)
    ba = jnp.where(lane < 2 * n_heads, ba_ref[0], 0.0)
    tt = ba.shape[0]
    beta = jax.nn.sigmoid(ba)
    g = -jnp.exp(alog_ref[...]) * jax.nn.softplus(ba + dt_ref[...])
    row = lax.broadcasted_iota(jnp.int32, (tt, tt), 0)
    col = lax.broadcasted_iota(jnp.int32, (tt, tt), 1)
    tri = jnp.where((row // CHUNK == col // CHUNK) & (col <= row), 1.0, 0.0).astype(F32)
    gc = jnp.dot(tri, g, preferred_element_type=F32, precision=HI)
    cols = jnp.where(lane < n_heads, beta, gc)
    cols_ref[0] = cols
    rows_ref[0] = cols.T


def _gdn_gates(ba, a_log, dt_bias, *, tt=256):
    b, t, _ = ba.shape
    hv = a_log.shape[0]
    assert 2 * hv <= LANES and t % tt == 0 and tt % CHUNK == 0
    pad = lambda v: jnp.zeros((1, LANES), F32).at[0, hv:2 * hv].set(v.astype(F32))
    return pl.pallas_call(
        functools.partial(_gdn_gates_kernel, n_heads=hv),
        grid=(b, t // tt),
        in_specs=[pl.BlockSpec((1, tt, LANES), lambda bi, i: (bi, i, 0)),
                  pl.BlockSpec((1, LANES), lambda bi, i: (0, 0)),
                  pl.BlockSpec((1, LANES), lambda bi, i: (0, 0))],
        out_specs=[pl.BlockSpec((1, tt, LANES), lambda bi, i: (bi, i, 0)),
                   pl.BlockSpec((1, LANES, tt), lambda bi, i: (bi, 0, i))],
        out_shape=[jax.ShapeDtypeStruct((b, t, LANES), F32),
                   jax.ShapeDtypeStruct((b, LANES, t), F32)],
        compiler_params=_cparams(("parallel", "parallel")),
        name="gdn_gates",
    )(ba, pad(a_log), pad(dt_bias))


def _gdn_core_kernel(q_ref, k_ref, v_ref, z_ref, wq_ref, wk_ref, wv_ref, gcol_ref, grow_ref, nw_ref, sel_ref,
                     o_ref, s_ref, qbuf, kbuf, vbuf, *, n_vheads):
    tt = q_ref.shape[1]
    n_g = q_ref.shape[2] // HEAD
    n_e = 2 * n_g
    hv0 = n_e * pl.program_id(1)

    @pl.when(pl.program_id(2) == 0)
    def _():
        s_ref[...] = jnp.zeros_like(s_ref)
        qbuf[...] = jnp.zeros_like(qbuf)
        kbuf[...] = jnp.zeros_like(kbuf)
        vbuf[...] = jnp.zeros_like(vbuf)

    def conv_silu(x_ref, w_ref, buf):
        xe = jnp.concatenate([buf[...], x_ref[0].astype(F32)], axis=0)
        w = w_ref[...]
        acc = w[0:1, :] * xe
        for j in range(1, CONV_K):
            acc = pltpu.roll(acc, 1, 0) + w[j:j + 1, :] * xe
        buf[...] = xe[tt:tt + HALO, :]
        return _silu(acc[HALO:, :])

    def l2norm_heads(x, scale):
        out = []
        for g in range(n_g):
            xg = x[:, g * HEAD:(g + 1) * HEAD]
            out.append(xg * (lax.rsqrt(jnp.sum(xg * xg, axis=-1, keepdims=True) + 1e-6) * scale))
        return out

    q = l2norm_heads(conv_silu(q_ref, wq_ref, qbuf), HEAD ** -0.5)
    k = l2norm_heads(conv_silu(k_ref, wk_ref, kbuf), 1.0)
    v = conv_silu(v_ref, wv_ref, vbuf)
    gcols = gcol_ref[0]
    nw = nw_ref[...]

    ci = lax.broadcasted_iota(jnp.int32, (CHUNK, CHUNK), 0)
    cj = lax.broadcasted_iota(jnp.int32, (CHUNK, CHUNK), 1)
    causal = cj <= ci
    strict = cj < ci
    eye = jnp.where(ci == cj, 1.0, 0.0).astype(F32)
    diag_blk = (ci // SUB) == (cj // SUB)
    level_masks = []
    size = SUB
    while size < CHUNK:
        level_masks.append(((ci // (2 * size)) == (cj // (2 * size))) & ((ci // size) != (cj // size)))
        size *= 2
    n_chunks = tt // CHUNK
    units = [(c, e) for c in range(n_chunks) for e in range(n_e)]

    g_hi = gcols.astype(BF16)
    g_r1 = gcols - g_hi.astype(F32)
    g_mid = g_r1.astype(BF16)
    g_lo = (g_r1 - g_mid.astype(F32)).astype(BF16)
    gates_b = _dot(jnp.concatenate([g_hi, g_mid, g_lo], axis=1), sel_ref[0])
    gc_rows = [grow_ref[0, pl.ds(n_vheads + hv0 + e, 1), :] for e in range(n_e)]

    q16 = [x.astype(BF16) for x in q]
    k16 = [x.astype(BF16) for x in k]
    qkk = {}
    for c in range(n_chunks):
        r = slice(c * CHUNK, (c + 1) * CHUNK)
        for g in range(n_g):
            qkk[c, g] = _dot_nt(jnp.concatenate([q16[g][r], k16[g][r]], axis=0), k16[g][r])

    p, tinv, off16, qk_m, rhs16, kdec16, qdec, gl = {}, {}, {}, {}, {}, {}, {}, {}
    for (c, e) in units:
        r = slice(c * CHUNK, (c + 1) * CHUNK)
        g = e // 2
        bcol = gates_b[r, e * LANES:(e + 1) * LANES]
        gcol = gates_b[r, (n_e + e) * LANES:(n_e + e + 1) * LANES]
        diff = gcol[:, :CHUNK] - gc_rows[e][:, r]
        decay = jnp.exp(jnp.where(causal, diff, NEG))
        a = jnp.where(strict, qkk[c, g][CHUNK:] * bcol[:, :CHUNK] * decay, 0.0)
        qk_m[c, e] = (qkk[c, g][:CHUNK] * decay).astype(BF16)
        p[c, e] = -jnp.where(diag_blk, a, 0.0)
        tinv[c, e] = eye + p[c, e]
        off16[c, e] = [jnp.where(m, a, 0.0).astype(BF16) for m in level_masks]
        eg = jnp.exp(gcol)
        kb = k[g][r] * bcol
        ve = v[r, e * HEAD:(e + 1) * HEAD]
        rhs16[c, e] = jnp.concatenate([kb * eg, ve * bcol], axis=1).astype(BF16)
        glast = gcol[CHUNK - 1:CHUNK, :]
        kdec16[c, e] = (k[g][r] * jnp.exp(glast - gcol)).astype(BF16)
        qdec[c, e] = q[g][r] * eg
        gl[c, e] = jnp.exp(glast)
    n_sq = int(math.log2(SUB)) - 1
    for un in units:
        p16 = p[un].astype(BF16)
        p[un] = _dot(p16, p16)
    for _ in range(n_sq - 1):
        for un in units:
            p16 = p[un].astype(BF16)
            tp = _dot(jnp.concatenate([tinv[un].astype(BF16), p16], axis=0), p16)
            tinv[un] = tinv[un] + tp[:CHUNK]
            p[un] = tp[CHUNK:]
    for un in units:
        tinv[un] = tinv[un] + _dot(tinv[un].astype(BF16), p[un].astype(BF16))
    for lvl in range(len(level_masks)):
        nmat = {}
        for un in units:
            nmat[un] = _dot(tinv[un].astype(BF16), off16[un][lvl]).astype(BF16)
        for un in units:
            tinv[un] = tinv[un] - _dot(nmat[un], tinv[un].astype(BF16))
    wu16 = {}
    for un in units:
        wu16[un] = _dot(tinv[un].astype(BF16), rhs16[un]).astype(BF16)
    lhs16, n_loc, o_loc = {}, {}, {}
    for un in units:
        qk_wu = _dot(qk_m[un], wu16[un])
        kd_wu = _dot_tn(kdec16[un], wu16[un])
        lhs16[un] = jnp.concatenate([kd_wu[:, :HEAD], qdec[un] - qk_wu[:, :HEAD]], axis=0).astype(BF16)
        n_loc[un] = kd_wu[:, HEAD:]
        o_loc[un] = qk_wu[:, HEAD:]

    s = [s_ref[e] for e in range(n_e)]
    for c in range(n_chunks):
        r = slice(c * CHUNK, (c + 1) * CHUNK)
        rs = [_dot(lhs16[c, e], s[e].astype(BF16)) for e in range(n_e)]
        for e in range(n_e):
            o = rs[e][HEAD:] + o_loc[c, e]
            s[e] = s[e] * gl[c, e] - rs[e][:HEAD] + n_loc[c, e]
            on = o * lax.rsqrt(jnp.mean(o * o, axis=-1, keepdims=True) + EPS) * nw
            ze = z_ref[0, r, e * HEAD:(e + 1) * HEAD].astype(F32)
            o_ref[0, r, e * HEAD:(e + 1) * HEAD] = (on * _silu(ze)).astype(o_ref.dtype)
    for e in range(n_e):
        s_ref[e] = s[e]


def _gdn_core(proj, conv_w, gcols, grows, norm_w, *, n_qheads, n_vheads, tt=256, group=4):
    b, t, _ = proj.shape
    group = min(group, n_qheads)
    assert n_vheads == 2 * n_qheads and n_qheads % group == 0 and t % tt == 0 and tt % CHUNK == 0
    qk_dim = n_qheads * HEAD
    v_dim = n_vheads * HEAD
    qw, vw = group * HEAD, 2 * group * HEAD
    n_groups = n_qheads // group
    assert (2 * qk_dim) % vw == 0 and (2 * qk_dim + v_dim) % vw == 0
    v_blk0 = (2 * qk_dim) // vw
    z_blk0 = (2 * qk_dim + v_dim) // vw
    n_e = 2 * group
    sel = np.zeros((n_groups, 3 * LANES, 2 * n_e * LANES), np.float32)
    for gi in range(n_groups):
        for e in range(n_e):
            for piece in range(3):
                sel[gi, piece * LANES + gi * n_e + e, e * LANES:(e + 1) * LANES] = 1.0
                sel[gi, piece * LANES + n_vheads + gi * n_e + e, (n_e + e) * LANES:(n_e + e + 1) * LANES] = 1.0
    return pl.pallas_call(
        functools.partial(_gdn_core_kernel, n_vheads=n_vheads),
        grid=(b, n_groups, t // tt),
        in_specs=[pl.BlockSpec((1, tt, qw), lambda bi, h, ti: (bi, ti, h)),
                  pl.BlockSpec((1, tt, qw), lambda bi, h, ti: (bi, ti, n_groups + h)),
                  pl.BlockSpec((1, tt, vw), lambda bi, h, ti: (bi, ti, v_blk0 + h)),
                  pl.BlockSpec((1, tt, vw), lambda bi, h, ti: (bi, ti, z_blk0 + h)),
                  pl.BlockSpec((CONV_K, qw), lambda bi, h, ti: (0, h)),
                  pl.BlockSpec((CONV_K, qw), lambda bi, h, ti: (0, n_groups + h)),
                  pl.BlockSpec((CONV_K, vw), lambda bi, h, ti: (0, v_blk0 + h)),
                  pl.BlockSpec((1, tt, LANES), lambda bi, h, ti: (bi, ti, 0)),
                  pl.BlockSpec((1, LANES, tt), lambda bi, h, ti: (bi, 0, ti)),
                  pl.BlockSpec((1, HEAD), lambda bi, h, ti: (0, 0)),
                  pl.BlockSpec((1, 3 * LANES, 2 * n_e * LANES), lambda bi, h, ti: (h, 0, 0))],
        out_specs=pl.BlockSpec((1, tt, vw), lambda bi, h, ti: (bi, ti, h)),
        out_shape=jax.ShapeDtypeStruct((b, t, v_dim), BF16),
        scratch_shapes=[pltpu.VMEM((2 * group, HEAD, HEAD), F32),
                        pltpu.VMEM((HALO, qw), F32),
                        pltpu.VMEM((HALO, qw), F32),
                        pltpu.VMEM((HALO, vw), F32)],
        compiler_params=_cparams(("parallel", "parallel", "arbitrary")),
        name="gdn_core",
    )(proj, proj, proj, proj, conv_w, conv_w, conv_w, gcols, grows, norm_w.reshape(1, HEAD), jnp.asarray(sel, BF16))


def _diff_attn_kernel(qi_tab, ki_tab, q1_ref, q2_ref, k1_ref, k2_ref, v_ref, lam_ref, sw_ref, o_ref,
                      m_ref, l_ref, acc_ref, *, n_heads, out_scale, lam_init):
    step = pl.program_id(2)
    h = pl.program_id(1)
    qi = qi_tab[step]
    ki = ki_tab[step]
    tq = q1_ref.shape[1]
    tk = k1_ref.shape[1]

    @pl.when(ki == 0)
    def _():
        m_ref[...] = jnp.full_like(m_ref, NEG)
        l_ref[...] = jnp.zeros_like(l_ref)
        acc_ref[...] = jnp.zeros_like(acc_ref)

    slope = jnp.exp2(-8.0 * (h + 1).astype(F32) / n_heads) * LOG2E
    off = ki * tk - qi * tq
    kpos = (lax.broadcasted_iota(jnp.int32, (1, tk), 1) + off).astype(F32)
    bias = slope * kpos

    def update(masked):
        v16 = v_ref[0]
        scores = [_dot_nt(q1_ref[0], k1_ref[0]) + bias, _dot_nt(q2_ref[0], k2_ref[0]) + bias]
        if masked:
            row = lax.broadcasted_iota(jnp.int32, (tq, tk), 0)
            col = lax.broadcasted_iota(jnp.int32, (tq, tk), 1)
            visible = (col + off) <= row
            scores = [jnp.where(visible, s, NEG) for s in scores]
        for c, s in enumerate(scores):
            m_old = m_ref[c]
            m_new = jnp.maximum(m_old, jnp.max(s, axis=-1, keepdims=True))
            p = jnp.exp2(s - jnp.concatenate([m_new] * (tk // LANES), axis=1))
            alpha = jnp.exp2(m_old - m_new)
            l_ref[c] = alpha * l_ref[c] + jnp.sum(p, axis=-1, keepdims=True)
            acc_ref[c] = (jnp.concatenate([alpha] * (acc_ref.shape[2] // LANES), axis=1) * acc_ref[c]
                          + _dot(p.astype(BF16), v16))
            m_ref[c] = m_new

    touches_diag = (ki + 1) * tk > qi * tq + 1
    pl.when(touches_diag)(functools.partial(update, True))
    pl.when(jnp.logical_not(touches_diag))(functools.partial(update, False))

    @pl.when((ki + 1) * tk >= (qi + 1) * tq)
    def _():
        lq1, lk1, lq2, lk2 = lam_ref[0:1, :], lam_ref[1:2, :], lam_ref[2:3, :], lam_ref[3:4, :]
        lam = (jnp.exp(jnp.sum(lq1 * lk1, axis=-1, keepdims=True))
               - jnp.exp(jnp.sum(lq2 * lk2, axis=-1, keepdims=True)) + lam_init)
        rep = acc_ref.shape[2] // LANES
        inv1 = jnp.concatenate([1.0 / l_ref[0]] * rep, axis=1)
        inv2 = jnp.concatenate([1.0 / l_ref[1]] * rep, axis=1)
        o = acc_ref[0] * inv1 - lam * (acc_ref[1] * inv2)
        on = o * lax.rsqrt(jnp.mean(o * o, axis=-1, keepdims=True) + EPS) * sw_ref[...]
        o_ref[0] = (on * out_scale).astype(o_ref.dtype)


def _diff_attn(q, kv, lam_vecs, subln_w, *, n_heads, layer_idx, tq=512, tk=512):
    b, t, qw = q.shape
    tq, tk = min(tq, t), min(tk, t)
    assert t % tq == 0 and tq % tk == 0
    dv = 2 * HEAD
    v_blk0 = qw // dv
    pairs = [(i, j) for i in range(t // tq) for j in range(((i + 1) * tq) // tk)]
    qi_tab = jnp.asarray(np.array([p[0] for p in pairs], np.int32))
    ki_tab = jnp.asarray(np.array([p[1] for p in pairs], np.int32))
    lam_init = 0.8 - 0.6 * math.exp(-0.3 * layer_idx)
    grid_spec = pltpu.PrefetchScalarGridSpec(
        num_scalar_prefetch=2,
        grid=(b, n_heads, len(pairs)),
        in_specs=[pl.BlockSpec((1, tq, HEAD), lambda bi, h, s, qt, kt: (bi, qt[s], 2 * h)),
                  pl.BlockSpec((1, tq, HEAD), lambda bi, h, s, qt, kt: (bi, qt[s], 2 * h + 1)),
                  pl.BlockSpec((1, tk, HEAD), lambda bi, h, s, qt, kt: (bi, kt[s], 2 * h)),
                  pl.BlockSpec((1, tk, HEAD), lambda bi, h, s, qt, kt: (bi, kt[s], 2 * h + 1)),
                  pl.BlockSpec((1, tk, dv), lambda bi, h, s, qt, kt: (bi, kt[s], v_blk0 + h)),
                  pl.BlockSpec((4, HEAD), lambda bi, h, s, qt, kt: (0, 0)),
                  pl.BlockSpec((1, dv), lambda bi, h, s, qt, kt: (0, 0))],
        out_specs=pl.BlockSpec((1, tq, dv), lambda bi, h, s, qt, kt: (bi, qt[s], h)),
        scratch_shapes=[pltpu.VMEM((2, tq, LANES), F32),
                        pltpu.VMEM((2, tq, LANES), F32),
                        pltpu.VMEM((2, tq, dv), F32)])
    return pl.pallas_call(
        functools.partial(_diff_attn_kernel, n_heads=n_heads, out_scale=1.0 - lam_init, lam_init=lam_init),
        grid_spec=grid_spec,
        out_shape=jax.ShapeDtypeStruct((b, t, n_heads * dv), BF16),
        compiler_params=_cparams(("parallel", "parallel", "arbitrary")),
        name="diff_attn",
    )(qi_tab, ki_tab, q, q, kv, kv, kv, lam_vecs, subln_w.reshape(1, dv))


def kernel(x, p, norm_mix, norm_mlp, norm_ple, gdn_w_in, gdn_conv_w, gdn_a_log, gdn_dt_bias, gdn_norm_w, gdn_w_out, kv_norm, w_kv, diff_w_q, diff_lambda_q1, diff_lambda_k1, diff_lambda_q2, diff_lambda_k2, diff_subln_w, diff_w_o, mlp_w1, mlp_w2, ple_w_proj, ple_w_gate, final_norm):
    b, t, d = x.shape
    m = b * t
    depth = norm_mix.shape[0]
    n_a = gdn_w_in.shape[0]
    n_vheads = gdn_a_log.shape[1]
    v_dim = n_vheads * HEAD
    conv_dim = gdn_conv_w.shape[2]
    qk_dim = (conv_dim - v_dim) // 2
    n_qheads = qk_dim // HEAD
    main_w = conv_dim + v_dim
    n_dheads = diff_w_q.shape[2] // (2 * HEAD)

    assert main_w % LANES == 0
    h = x.reshape(m, d)
    p2 = p.reshape(depth, m, -1)
    kv = None
    for i in range(depth):
        if i < n_a:
            proj = _norm_matmul(h, norm_mix[i], gdn_w_in, i, n=main_w)
            ba = _norm_matmul(h, norm_mix[i], gdn_w_in, i, n=LANES, col0=main_w // LANES, out_dtype=F32)
            gcols, grows = _gdn_gates(ba.reshape(b, t, LANES), gdn_a_log[i], gdn_dt_bias[i])
            o = _gdn_core(proj.reshape(b, t, main_w), gdn_conv_w[i], gcols, grows,
                          gdn_norm_w[i], n_qheads=n_qheads, n_vheads=n_vheads)
            h = _matmul_residual(o.reshape(m, v_dim), gdn_w_out, i, h)
        else:
            j = i - n_a
            q = _norm_matmul(h, norm_mix[i], diff_w_q, j, scale=HEAD ** -0.5 * LOG2E)
            lam_vecs = jnp.stack([diff_lambda_q1[j], diff_lambda_k1[j],
                                  diff_lambda_q2[j], diff_lambda_k2[j]]).astype(F32)
            o = _diff_attn(q.reshape(b, t, -1), kv, lam_vecs, diff_subln_w[j],
                           n_heads=n_dheads, layer_idx=i)
            h = _matmul_residual(o.reshape(m, -1), diff_w_o, j, h)
        a = _norm_matmul(h, norm_mlp[i], mlp_w1, i, act="relu2")
        h = _matmul_residual(a, mlp_w2, i, h)
        h = _ple(h, norm_ple[i], ple_w_gate, p2, ple_w_proj, i, final_norm, final_norm=(i == depth - 1))
        if i == n_a - 1:
            kv = _norm_matmul(h, kv_norm, w_kv[None], 0).reshape(b, t, -1)
    return h.reshape(b, t, d)
```

```python
import functools
import math

import numpy as np
import jax
import jax.numpy as jnp
from jax import lax
from jax.experimental import pallas as pl
from jax.experimental.pallas import tpu as pltpu

F32 = jnp.float32
BF16 = jnp.bfloat16
EPS = 1e-6
LANES = 128
HEAD = 128
CHUNK = 64
SUB = 16
CONV_K = 4
HALO = 8
NEG = -1e30
LOG2E = 1.4426950408889634
VMEM_LIMIT = 56 * 1024 * 1024
HI = lax.Precision.HIGHEST


def _cparams(sem):
    return pltpu.CompilerParams(dimension_semantics=sem, vmem_limit_bytes=VMEM_LIMIT)


def _tile(n, target):
    if n <= target:
        return n
    t = target - target % LANES
    while n % t:
        t -= LANES
    assert t > 0
    return t


def _dot(a, b):
    return jnp.dot(a, b, preferred_element_type=F32)


def _dot_nt(a, b):
    return lax.dot_general(a, b, (((1,), (1,)), ((), ())), preferred_element_type=F32)


def _dot_tn(a, b):
    return lax.dot_general(a, b, (((0,), (0,)), ((), ())), preferred_element_type=F32)


def _silu(x):
    hx = 0.5 * x
    return hx + hx * jnp.tanh(hx)


def _norm_matmul_kernel(x_ref, g_ref, w_ref, o_ref, xn_ref, *, act, scale, n_valid):
    @pl.when(pl.program_id(1) == 0)
    def _():
        x = x_ref[...]
        ms = jnp.mean(x * x, axis=-1, keepdims=True)
        xn_ref[...] = (x * lax.rsqrt(ms + EPS) * g_ref[...]).astype(BF16)

    w = w_ref[...]
    if n_valid is not None:
        col = lax.broadcasted_iota(jnp.int32, w.shape, 1)
        w = jnp.where(col < n_valid, w, jnp.zeros_like(w))
    acc = _dot(xn_ref[...], w)
    if act == "relu2":
        acc = jnp.square(jnp.maximum(acc, 0.0))
    if scale != 1.0:
        acc = acc * scale
    o_ref[...] = acc.astype(o_ref.dtype)


def _norm_matmul(x, gain, w, layer, *, n=None, col0=0, n_valid=None, act=None, scale=1.0, out_dtype=BF16,
                 tm=1024, tn=2048):
    m, d = x.shape
    n = w.shape[2] if n is None else n
    tm, tn = _tile(m, tm), _tile(n, tn)
    assert n_valid is None or n == tn
    return pl.pallas_call(
        functools.partial(_norm_matmul_kernel, act=act, scale=scale, n_valid=n_valid),
        grid=(m // tm, n // tn),
        in_specs=[pl.BlockSpec((tm, d), lambda i, j: (i, 0)),
                  pl.BlockSpec((1, d), lambda i, j: (0, 0)),
                  pl.BlockSpec((None, d, tn), lambda i, j: (layer, 0, col0 + j))],
        out_specs=pl.BlockSpec((tm, tn), lambda i, j: (i, j)),
        out_shape=jax.ShapeDtypeStruct((m, n), out_dtype),
        scratch_shapes=[pltpu.VMEM((tm, d), BF16)],
        compiler_params=_cparams(("parallel", "arbitrary")),
        name="norm_matmul",
    )(x, gain.reshape(1, d), w)


def _matmul_residual_kernel(a_ref, w_ref, h_ref, o_ref):
    @pl.when(pl.program_id(2) == 0)
    def _():
        o_ref[...] = h_ref[...]

    o_ref[...] += _dot(a_ref[...], w_ref[...])


def _matmul_residual(a, w, layer, h, *, tm=1024, tn=1024, tk=4096):
    m, k = a.shape
    n = w.shape[2]
    tm, tn, tk = _tile(m, tm), _tile(n, tn), _tile(k, tk)
    return pl.pallas_call(
        _matmul_residual_kernel,
        grid=(m // tm, n // tn, k // tk),
        in_specs=[pl.BlockSpec((tm, tk), lambda i, j, kk: (i, kk)),
                  pl.BlockSpec((None, tk, tn), lambda i, j, kk: (layer, kk, j)),
                  pl.BlockSpec((tm, tn), lambda i, j, kk: (i, j))],
        out_specs=pl.BlockSpec((tm, tn), lambda i, j, kk: (i, j)),
        out_shape=jax.ShapeDtypeStruct((m, n), F32),
        compiler_params=_cparams(("parallel", "parallel", "arbitrary")),
        name="matmul_residual",
    )(a, w, h)


def _ple_kernel(h_ref, g_ref, wg_ref, p_ref, wp_ref, fg_ref, o_ref, wg16, wp16, *, final_norm):
    @pl.when(pl.program_id(0) == 0)
    def _():
        wg16[...] = wg_ref[...].astype(BF16)
        wp16[...] = wp_ref[...].astype(BF16)

    h = h_ref[...]
    ms = jnp.mean(h * h, axis=-1, keepdims=True)
    hn = (h * lax.rsqrt(ms + EPS) * g_ref[...]).astype(BF16)
    gate = jax.nn.sigmoid(_dot(hn, wg16[...]))
    emb = _dot(p_ref[...].astype(BF16), wp16[...])
    out = h + emb * gate
    if final_norm:
        out = out * lax.rsqrt(jnp.mean(out * out, axis=-1, keepdims=True) + EPS) * fg_ref[...]
    o_ref[...] = out


def _ple(h, gain, w_gate, p, w_proj, layer, final_gain, *, final_norm, tm=512):
    m, d = h.shape
    e = p.shape[2]
    tm = min(tm, m)
    assert m % tm == 0
    once = pl.Buffered(1)
    return pl.pallas_call(
        functools.partial(_ple_kernel, final_norm=final_norm),
        grid=(m // tm,),
        in_specs=[pl.BlockSpec((tm, d), lambda i: (i, 0)),
                  pl.BlockSpec((1, d), lambda i: (0, 0)),
                  pl.BlockSpec((None, d, d), lambda i: (layer, 0, 0), pipeline_mode=once),
                  pl.BlockSpec((None, tm, e), lambda i: (layer, i, 0)),
                  pl.BlockSpec((None, e, d), lambda i: (layer, 0, 0), pipeline_mode=once),
                  pl.BlockSpec((1, d), lambda i: (0, 0))],
        out_specs=pl.BlockSpec((tm, d), lambda i: (i, 0)),
        out_shape=jax.ShapeDtypeStruct((m, d), F32),
        scratch_shapes=[pltpu.VMEM((d, d), BF16), pltpu.VMEM((e, d), BF16)],
        compiler_params=_cparams(("arbitrary",)),
        name="ple",
    )(h, gain.reshape(1, d), w_gate, p, w_proj, final_gain.reshape(1, d))


def _gdn_gates_kernel(ba_ref, alog_ref, dt_ref, cols_ref, rows_ref, *, n_heads):
    ba = ba_ref[0]
    lane = lax.broadcasted_iota(jnp.int32, ba.shape, 1)
    tt = ba.shape[0]
    beta = jax.nn.sigmoid(ba)
    g = -jnp.exp(alog_ref[...]) * jax.nn.softplus(ba + dt_ref[...])
    row = lax.broadcasted_iota(jnp.int32, (tt, tt), 0)
    col = lax.broadcasted_iota(jnp.int32, (tt, tt), 1)
    tri = jnp.where((row // CHUNK == col // CHUNK) & (col <= row), 1.0, 0.0).astype(F32)
    gc = jnp.dot(tri, g, preferred_element_type=F32, precision=HI)
    cols = jnp.where(lane < n_heads, beta, gc)
    cols_ref[0] = cols
    rows_ref[0] = cols.T


def _gdn_gates(ba, a_log, dt_bias, *, tt=256):
    b, t, _ = ba.shape
    hv = a_log.shape[0]
    assert 2 * hv <= LANES and t % tt == 0 and tt % CHUNK == 0
    pad = lambda v: jnp.zeros((1, LANES), F32).at[0, hv:2 * hv].set(v.astype(F32))
    return pl.pallas_call(
        functools.partial(_gdn_gates_kernel, n_heads=hv),
        grid=(b, t // tt),
        in_specs=[pl.BlockSpec((1, tt, LANES), lambda bi, i: (bi, i, 0)),
                  pl.BlockSpec((1, LANES), lambda bi, i: (0, 0)),
                  pl.BlockSpec((1, LANES), lambda bi, i: (0, 0))],
        out_specs=[pl.BlockSpec((1, tt, LANES), lambda bi, i: (bi, i, 0)),
                   pl.BlockSpec((1, LANES, tt), lambda bi, i: (bi, 0, i))],
        out_shape=[jax.ShapeDtypeStruct((b, t, LANES), F32),
                   jax.ShapeDtypeStruct((b, LANES, t), F32)],
        compiler_params=_cparams(("parallel", "parallel")),
        name="gdn_gates",
    )(ba, pad(a_log), pad(dt_bias))


def _gdn_core_kernel(q_ref, k_ref, v_ref, z_ref, wq_ref, wk_ref, wv_ref, gcol_ref, grow_ref, nw_ref, sel_ref,
                     o_ref, s_ref, qbuf, kbuf, vbuf, *, n_vheads):
    tt = q_ref.shape[1]
    n_g = q_ref.shape[2] // HEAD
    n_e = 2 * n_g
    hv0 = n_e * pl.program_id(1)

    @pl.when(pl.program_id(2) == 0)
    def _():
        s_ref[...] = jnp.zeros_like(s_ref)
        qbuf[...] = jnp.zeros_like(qbuf)
        kbuf[...] = jnp.zeros_like(kbuf)
        vbuf[...] = jnp.zeros_like(vbuf)

    def conv_silu(x_ref, w_ref, buf):
        xe = jnp.concatenate([buf[...], x_ref[0].astype(F32)], axis=0)
        w = w_ref[...]
        acc = w[0:1, :] * xe
        for j in range(1, CONV_K):
            acc = pltpu.roll(acc, 1, 0) + w[j:j + 1, :] * xe
        buf[...] = xe[tt:tt + HALO, :]
        return _silu(acc[HALO:, :])

    def l2norm_heads(x, scale):
        out = []
        for g in range(n_g):
            xg = x[:, g * HEAD:(g + 1) * HEAD]
            out.append(xg * (lax.rsqrt(jnp.sum(xg * xg, axis=-1, keepdims=True) + 1e-6) * scale))
        return out

    q = l2norm_heads(conv_silu(q_ref, wq_ref, qbuf), HEAD ** -0.5)
    k = l2norm_heads(conv_silu(k_ref, wk_ref, kbuf), 1.0)
    v = conv_silu(v_ref, wv_ref, vbuf)
    gcols = gcol_ref[0]
    nw = nw_ref[...]

    ci = lax.broadcasted_iota(jnp.int32, (CHUNK, CHUNK), 0)
    cj = lax.broadcasted_iota(jnp.int32, (CHUNK, CHUNK), 1)
    causal = cj <= ci
    strict = cj < ci
    eye = jnp.where(ci == cj, 1.0, 0.0).astype(F32)
    diag_blk = (ci // SUB) == (cj // SUB)
    level_masks = []
    size = SUB
    while size < CHUNK:
        level_masks.append(((ci // (2 * size)) == (cj // (2 * size))) & ((ci // size) != (cj // size)))
        size *= 2
    n_chunks = tt // CHUNK
    units = [(c, e) for c in range(n_chunks) for e in range(n_e)]

    g_hi = gcols.astype(BF16)
    g_r1 = gcols - g_hi.astype(F32)
    g_mid = g_r1.astype(BF16)
    g_lo = (g_r1 - g_mid.astype(F32)).astype(BF16)
    gates_b = _dot(jnp.concatenate([g_hi, g_mid, g_lo], axis=1), sel_ref[0])
    gc_rows = [grow_ref[0, pl.ds(n_vheads + hv0 + e, 1), :] for e in range(n_e)]

    q16 = [x.astype(BF16) for x in q]
    k16 = [x.astype(BF16) for x in k]
    qkk = {}
    for c in range(n_chunks):
        r = slice(c * CHUNK, (c + 1) * CHUNK)
        for g in range(n_g):
            qkk[c, g] = _dot_nt(jnp.concatenate([q16[g][r], k16[g][r]], axis=0), k16[g][r])

    p, tinv, off16, qk_m, rhs16, kdec16, qdec, gl = {}, {}, {}, {}, {}, {}, {}, {}
    for (c, e) in units:
        r = slice(c * CHUNK, (c + 1) * CHUNK)
        g = e // 2
        bcol = gates_b[r, e * LANES:(e + 1) * LANES]
        gcol = gates_b[r, (n_e + e) * LANES:(n_e + e + 1) * LANES]
        diff = gcol[:, :CHUNK] - gc_rows[e][:, r]
        decay = jnp.exp(jnp.where(causal, diff, NEG))
        a = jnp.where(strict, qkk[c, g][CHUNK:] * bcol[:, :CHUNK] * decay, 0.0)
        qk_m[c, e] = (qkk[c, g][:CHUNK] * decay).astype(BF16)
        p[c, e] = -jnp.where(diag_blk, a, 0.0)
        tinv[c, e] = eye + p[c, e]
        off16[c, e] = [jnp.where(m, a, 0.0).astype(BF16) for m in level_masks]
        eg = jnp.exp(gcol)
        kb = k[g][r] * bcol
        ve = v[r, e * HEAD:(e + 1) * HEAD]
        rhs16[c, e] = jnp.concatenate([kb * eg, ve * bcol], axis=1).astype(BF16)
        glast = gcol[CHUNK - 1:CHUNK, :]
        kdec16[c, e] = (k[g][r] * jnp.exp(glast - gcol)).astype(BF16)
        qdec[c, e] = q[g][r] * eg
        gl[c, e] = jnp.exp(glast)
    n_sq = int(math.log2(SUB)) - 1
    for un in units:
        p16 = p[un].astype(BF16)
        p[un] = _dot(p16, p16)
    for _ in range(n_sq - 1):
        for un in units:
            p16 = p[un].astype(BF16)
            tp = _dot(jnp.concatenate([tinv[un].astype(BF16), p16], axis=0), p16)
            tinv[un] = tinv[un] + tp[:CHUNK]
            p[un] = tp[CHUNK:]
    for un in units:
        tinv[un] = tinv[un] + _dot(tinv[un].astype(BF16), p[un].astype(BF16))
    for lvl in range(len(level_masks)):
        nmat = {}
        for un in units:
            nmat[un] = _dot(tinv[un].astype(BF16), off16[un][lvl]).astype(BF16)
        for un in units:
            tinv[un] = tinv[un] - _dot(nmat[un], tinv[un].astype(BF16))
    wu16 = {}
    for un in units:
        wu16[un] = _dot(tinv[un].astype(BF16), rhs16[un]).astype(BF16)
    lhs16, n_loc, o_loc = {}, {}, {}
    for un in units:
        qk_wu = _dot(qk_m[un], wu16[un])
        kd_wu = _dot_tn(kdec16[un], wu16[un])
        lhs16[un] = jnp.concatenate([kd_wu[:, :HEAD], qdec[un] - qk_wu[:, :HEAD]], axis=0).astype(BF16)
        n_loc[un] = kd_wu[:, HEAD:]
        o_loc[un] = qk_wu[:, HEAD:]

    s = [s_ref[e] for e in range(n_e)]
    for c in range(n_chunks):
        r = slice(c * CHUNK, (c + 1) * CHUNK)
        rs = [_dot(lhs16[c, e], s[e].astype(BF16)) for e in range(n_e)]
        for e in range(n_e):
            o = rs[e][HEAD:] + o_loc[c, e]
            s[e] = s[e] * gl[c, e] - rs[e][:HEAD] + n_loc[c, e]
            on = o * lax.rsqrt(jnp.mean(o * o, axis=-1, keepdims=True) + EPS) * nw
            ze = z_ref[0, r, e * HEAD:(e + 1) * HEAD].astype(F32)
            o_ref[0, r, e * HEAD:(e + 1) * HEAD] = (on * _silu(ze)).astype(o_ref.dtype)
    for e in range(n_e):
        s_ref[e] = s[e]


def _gdn_core(proj, conv_w, gcols, grows, norm_w, *, n_qheads, n_vheads, tt=256, group=4):
    b, t, _ = proj.shape
    group = min(group, n_qheads)
    assert n_vheads == 2 * n_qheads and n_qheads % group == 0 and t % tt == 0 and tt % CHUNK == 0
    qk_dim = n_qheads * HEAD
    v_dim = n_vheads * HEAD
    qw, vw = group * HEAD, 2 * group * HEAD
    n_groups = n_qheads // group
    assert (2 * qk_dim) % vw == 0 and (2 * qk_dim + v_dim) % vw == 0
    v_blk0 = (2 * qk_dim) // vw
    z_blk0 = (2 * qk_dim + v_dim) // vw
    n_e = 2 * group
    sel = np.zeros((n_groups, 3 * LANES, 2 * n_e * LANES), np.float32)
    for gi in range(n_groups):
        for e in range(n_e):
            for piece in range(3):
                sel[gi, piece * LANES + gi * n_e + e, e * LANES:(e + 1) * LANES] = 1.0
                sel[gi, piece * LANES + n_vheads + gi * n_e + e, (n_e + e) * LANES:(n_e + e + 1) * LANES] = 1.0
    return pl.pallas_call(
        functools.partial(_gdn_core_kernel, n_vheads=n_vheads),
        grid=(b, n_groups, t // tt),
        in_specs=[pl.BlockSpec((1, tt, qw), lambda bi, h, ti: (bi, ti, h)),
                  pl.BlockSpec((1, tt, qw), lambda bi, h, ti: (bi, ti, n_groups + h)),
                  pl.BlockSpec((1, tt, vw), lambda bi, h, ti: (bi, ti, v_blk0 + h)),
                  pl.BlockSpec((1, tt, vw), lambda bi, h, ti: (bi, ti, z_blk0 + h)),
                  pl.BlockSpec((CONV_K, qw), lambda bi, h, ti: (0, h)),
                  pl.BlockSpec((CONV_K, qw), lambda bi, h, ti: (0, n_groups + h)),
                  pl.BlockSpec((CONV_K, vw), lambda bi, h, ti: (0, v_blk0 + h)),
                  pl.BlockSpec((1, tt, LANES), lambda bi, h, ti: (bi, ti, 0)),
                  pl.BlockSpec((1, LANES, tt), lambda bi, h, ti: (bi, 0, ti)),
                  pl.BlockSpec((1, HEAD), lambda bi, h, ti: (0, 0)),
                  pl.BlockSpec((1, 3 * LANES, 2 * n_e * LANES), lambda bi, h, ti: (h, 0, 0))],
        out_specs=pl.BlockSpec((1, tt, vw), lambda bi, h, ti: (bi, ti, h)),
        out_shape=jax.ShapeDtypeStruct((b, t, v_dim), BF16),
        scratch_shapes=[pltpu.VMEM((2 * group, HEAD, HEAD), F32),
                        pltpu.VMEM((HALO, qw), F32),
                        pltpu.VMEM((HALO, qw), F32),
                        pltpu.VMEM((HALO, vw), F32)],
        compiler_params=_cparams(("parallel", "parallel", "arbitrary")),
        name="gdn_core",
    )(proj, proj, proj, proj, conv_w, conv_w, conv_w, gcols, grows, norm_w.reshape(1, HEAD), jnp.asarray(sel, BF16))


def _diff_attn_kernel(qi_tab, ki_tab, q1_ref, q2_ref, k1_ref, k2_ref, v_ref, lam_ref, sw_ref, o_ref,
                      m_ref, l_ref, acc_ref, *, n_heads, out_scale, lam_init):
    step = pl.program_id(2)
    h = pl.program_id(1)
    qi = qi_tab[step]
    ki = ki_tab[step]
    tq = q1_ref.shape[1]
    tk = k1_ref.shape[1]

    @pl.when(ki == 0)
    def _():
        m_ref[...] = jnp.full_like(m_ref, NEG)
        l_ref[...] = jnp.zeros_like(l_ref)
        acc_ref[...] = jnp.zeros_like(acc_ref)

    slope = jnp.exp2(-8.0 * (h + 1).astype(F32) / n_heads) * LOG2E
    off = ki * tk - qi * tq
    kpos = (lax.broadcasted_iota(jnp.int32, (1, tk), 1) + off).astype(F32)
    bias = slope * kpos

    def update(masked):
        v16 = v_ref[0]
        scores = [_dot_nt(q1_ref[0], k1_ref[0]) + bias, _dot_nt(q2_ref[0], k2_ref[0]) + bias]
        if masked:
            row = lax.broadcasted_iota(jnp.int32, (tq, tk), 0)
            col = lax.broadcasted_iota(jnp.int32, (tq, tk), 1)
            visible = (col + off) <= row
            scores = [jnp.where(visible, s, NEG) for s in scores]
        for c, s in enumerate(scores):
            m_old = m_ref[c]
            m_new = jnp.maximum(m_old, jnp.max(s, axis=-1, keepdims=True))
            p = jnp.exp2(s - jnp.concatenate([m_new] * (tk // LANES), axis=1))
            alpha = jnp.exp2(m_old - m_new)
            l_ref[c] = alpha * l_ref[c] + jnp.sum(p, axis=-1, keepdims=True)
            acc_ref[c] = (jnp.concatenate([alpha] * (acc_ref.shape[2] // LANES), axis=1) * acc_ref[c]
                          + _dot(p.astype(BF16), v16))
            m_ref[c] = m_new

    touches_diag = (ki + 1) * tk > qi * tq + 1
    pl.when(touches_diag)(functools.partial(update, True))
    pl.when(jnp.logical_not(touches_diag))(functools.partial(update, False))

    @pl.when((ki + 1) * tk >= (qi + 1) * tq)
    def _():
        lq1, lk1, lq2, lk2 = lam_ref[0:1, :], lam_ref[1:2, :], lam_ref[2:3, :], lam_ref[3:4, :]
        lam = (jnp.exp(jnp.sum(lq1 * lk1, axis=-1, keepdims=True))
               - jnp.exp(jnp.sum(lq2 * lk2, axis=-1, keepdims=True)) + lam_init)
        rep = acc_ref.shape[2] // LANES
        inv1 = jnp.concatenate([1.0 / l_ref[0]] * rep, axis=1)
        inv2 = jnp.concatenate([1.0 / l_ref[1]] * rep, axis=1)
        o = acc_ref[0] * inv1 - lam * (acc_ref[1] * inv2)
        on = o * lax.rsqrt(jnp.mean(o * o, axis=-1, keepdims=True) + EPS) * sw_ref[...]
        o_ref[0] = (on * out_scale).astype(o_ref.dtype)


def _diff_attn(q, kv, lam_vecs, subln_w, *, n_heads, layer_idx, tq=512, tk=512):
    b, t, qw = q.shape
    tq, tk = min(tq, t), min(tk, t)
    assert t % tq == 0 and tq % tk == 0
    dv = 2 * HEAD
    v_blk0 = qw // dv
    pairs = [(i, j) for i in range(t // tq) for j in range(((i + 1) * tq) // tk)]
    qi_tab = jnp.asarray(np.array([p[0] for p in pairs], np.int32))
    ki_tab = jnp.asarray(np.array([p[1] for p in pairs], np.int32))
    lam_init = 0.8 - 0.6 * math.exp(-0.3 * layer_idx)
    grid_spec = pltpu.PrefetchScalarGridSpec(
        num_scalar_prefetch=2,
        grid=(b, n_heads, len(pairs)),
        in_specs=[pl.BlockSpec((1, tq, HEAD), lambda bi, h, s, qt, kt: (bi, qt[s], 2 * h)),
                  pl.BlockSpec((1, tq, HEAD), lambda bi, h, s, qt, kt: (bi, qt[s], 2 * h + 1)),
                  pl.BlockSpec((1, tk, HEAD), lambda bi, h, s, qt, kt: (bi, kt[s], 2 * h)),
                  pl.BlockSpec((1, tk, HEAD), lambda bi, h, s, qt, kt: (bi, kt[s], 2 * h + 1)),
                  pl.BlockSpec((1, tk, dv), lambda bi, h, s, qt, kt: (bi, kt[s], v_blk0 + h)),
                  pl.BlockSpec((4, HEAD), lambda bi, h, s, qt, kt: (0, 0)),
                  pl.BlockSpec((1, dv), lambda bi, h, s, qt, kt: (0, 0))],
        out_specs=pl.BlockSpec((1, tq, dv), lambda bi, h, s, qt, kt: (bi, qt[s], h)),
        scratch_shapes=[pltpu.VMEM((2, tq, LANES), F32),
                        pltpu.VMEM((2, tq, LANES), F32),
                        pltpu.VMEM((2, tq, dv), F32)])
    return pl.pallas_call(
        functools.partial(_diff_attn_kernel, n_heads=n_heads, out_scale=1.0 - lam_init, lam_init=lam_init),
        grid_spec=grid_spec,
        out_shape=jax.ShapeDtypeStruct((b, t, n_heads * dv), BF16),
        compiler_params=_cparams(("parallel", "parallel", "arbitrary")),
        name="diff_attn",
    )(qi_tab, ki_tab, q, q, kv, kv, kv, lam_vecs, subln_w.reshape(1, dv))


def kernel(x, p, norm_mix, norm_mlp, norm_ple, gdn_w_in, gdn_conv_w, gdn_a_log, gdn_dt_bias, gdn_norm_w, gdn_w_out, kv_norm, w_kv, diff_w_q, diff_lambda_q1, diff_lambda_k1, diff_lambda_q2, diff_lambda_k2, diff_subln_w, diff_w_o, mlp_w1, mlp_w2, ple_w_proj, ple_w_gate, final_norm):
    b, t, d = x.shape
    m = b * t
    depth = norm_mix.shape[0]
    n_a = gdn_w_in.shape[0]
    n_vheads = gdn_a_log.shape[1]
    v_dim = n_vheads * HEAD
    conv_dim = gdn_conv_w.shape[2]
    qk_dim = (conv_dim - v_dim) // 2
    n_qheads = qk_dim // HEAD
    main_w = conv_dim + v_dim
    n_dheads = diff_w_q.shape[2] // (2 * HEAD)

    assert main_w % LANES == 0
    h = x.reshape(m, d)
    p2 = p.reshape(depth, m, -1)
    gdn_w_in, gdn_w_out, w_kv, diff_w_q, diff_w_o, mlp_w1, mlp_w2 = (
        w.astype(BF16) for w in (gdn_w_in, gdn_w_out, w_kv, diff_w_q, diff_w_o, mlp_w1, mlp_w2))
    kv = None
    for i in range(depth):
        if i < n_a:
            proj = _norm_matmul(h, norm_mix[i], gdn_w_in, i, n=main_w)
            ba = _norm_matmul(h, norm_mix[i], gdn_w_in, i, n=LANES, col0=main_w // LANES, n_valid=2 * n_vheads,
                              out_dtype=F32)
            gcols, grows = _gdn_gates(ba.reshape(b, t, LANES), gdn_a_log[i], gdn_dt_bias[i])
            o = _gdn_core(proj.reshape(b, t, main_w), gdn_conv_w[i], gcols, grows,
                          gdn_norm_w[i], n_qheads=n_qheads, n_vheads=n_vheads)
            h = _matmul_residual(o.reshape(m, v_dim), gdn_w_out, i, h)
        else:
            j = i - n_a
            q = _norm_matmul(h, norm_mix[i], diff_w_q, j, scale=HEAD ** -0.5 * LOG2E)
            lam_vecs = jnp.stack([diff_lambda_q1[j], diff_lambda_k1[j],
                                  diff_lambda_q2[j], diff_lambda_k2[j]]).astype(F32)
            o = _diff_attn(q.reshape(b, t, -1), kv, lam_vecs, diff_subln_w[j],
                           n_heads=n_dheads, layer_idx=i)
            h = _matmul_residual(o.reshape(m, -1), diff_w_o, j, h)
        a = _norm_matmul(h, norm_mlp[i], mlp_w1, i, act="relu2")
        h = _matmul_residual(a, mlp_w2, i, h)
        h = _ple(h, norm_ple[i], ple_w_gate, p2, ple_w_proj, i, final_norm, final_norm=(i == depth - 1))
        if i == n_a - 1:
            kv = _norm_matmul(h, kv_norm, w_kv[None], 0).reshape(b, t, -1)
    return h.reshape(b, t, d)
```

```python
import functools
import math

import numpy as np
import jax
import jax.numpy as jnp
from jax import lax
from jax.experimental import pallas as pl
from jax.experimental.pallas import tpu as pltpu

F32 = jnp.float32
BF16 = jnp.bfloat16
EPS = 1e-6
LANES = 128
HEAD = 128
CHUNK = 64
SUB = 16
CONV_K = 4
HALO = 8
NEG = -1e30
LOG2E = 1.4426950408889634
VMEM_LIMIT = 56 * 1024 * 1024
HI = lax.Precision.HIGHEST


def _cparams(sem):
    return pltpu.CompilerParams(dimension_semantics=sem, vmem_limit_bytes=VMEM_LIMIT)


def _tile(n, target):
    if n <= target:
        return n
    t = target - target % LANES
    while n % t:
        t -= LANES
    assert t > 0
    return t


def _dot(a, b):
    return jnp.dot(a, b, preferred_element_type=F32)


def _dot_nt(a, b):
    return lax.dot_general(a, b, (((1,), (1,)), ((), ())), preferred_element_type=F32)


def _dot_tn(a, b):
    return lax.dot_general(a, b, (((0,), (0,)), ((), ())), preferred_element_type=F32)


def _silu(x):
    hx = 0.5 * x
    return hx + hx * jnp.tanh(hx)


def _norm_matmul_kernel(x_ref, g_ref, w_ref, o_ref, xn_ref, *, act, scale, n_valid):
    @pl.when(pl.program_id(1) == 0)
    def _():
        x = x_ref[...]
        ms = jnp.mean(x * x, axis=-1, keepdims=True)
        xn_ref[...] = (x * lax.rsqrt(ms + EPS) * g_ref[...]).astype(BF16)

    w = w_ref[...]
    if n_valid is not None:
        col = lax.broadcasted_iota(jnp.int32, w.shape, 1)
        w = jnp.where(col < n_valid, w, jnp.zeros_like(w))
    acc = _dot(xn_ref[...], w)
    if act == "relu2":
        acc = jnp.square(jnp.maximum(acc, 0.0))
    if scale != 1.0:
        acc = acc * scale
    o_ref[...] = acc.astype(o_ref.dtype)


def _norm_matmul(x, gain, w, layer, *, n=None, col0=0, n_valid=None, act=None, scale=1.0, out_dtype=BF16,
                 tm=1024, tn=2048):
    m, d = x.shape
    n = w.shape[2] if n is None else n
    tm, tn = _tile(m, tm), _tile(n, tn)
    assert n_valid is None or n == tn
    return pl.pallas_call(
        functools.partial(_norm_matmul_kernel, act=act, scale=scale, n_valid=n_valid),
        grid=(m // tm, n // tn),
        in_specs=[pl.BlockSpec((tm, d), lambda i, j: (i, 0)),
                  pl.BlockSpec((1, d), lambda i, j: (0, 0)),
                  pl.BlockSpec((None, d, tn), lambda i, j: (layer, 0, col0 + j))],
        out_specs=pl.BlockSpec((tm, tn), lambda i, j: (i, j)),
        out_shape=jax.ShapeDtypeStruct((m, n), out_dtype),
        scratch_shapes=[pltpu.VMEM((tm, d), BF16)],
        compiler_params=_cparams(("parallel", "arbitrary")),
        name="norm_matmul",
    )(x, gain.reshape(1, d), w)


def _matmul_residual_kernel(a_ref, w_ref, h_ref, o_ref):
    @pl.when(pl.program_id(2) == 0)
    def _():
        o_ref[...] = h_ref[...]

    o_ref[...] += _dot(a_ref[...], w_ref[...])


def _matmul_residual(a, w, layer, h, *, tm=1024, tn=1024, tk=4096):
    m, k = a.shape
    n = w.shape[2]
    tm, tn, tk = _tile(m, tm), _tile(n, tn), _tile(k, tk)
    return pl.pallas_call(
        _matmul_residual_kernel,
        grid=(m // tm, n // tn, k // tk),
        in_specs=[pl.BlockSpec((tm, tk), lambda i, j, kk: (i, kk)),
                  pl.BlockSpec((None, tk, tn), lambda i, j, kk: (layer, kk, j)),
                  pl.BlockSpec((tm, tn), lambda i, j, kk: (i, j))],
        out_specs=pl.BlockSpec((tm, tn), lambda i, j, kk: (i, j)),
        out_shape=jax.ShapeDtypeStruct((m, n), F32),
        compiler_params=_cparams(("parallel", "parallel", "arbitrary")),
        name="matmul_residual",
    )(a, w, h)


def _ple_kernel(h_ref, g_ref, wg_ref, p_ref, wp_ref, fg_ref, o_ref, hn_ref, *, final_norm):
    h = h_ref[...]
    ms = jnp.mean(h * h, axis=-1, keepdims=True)
    hn_ref[...] = (h * lax.rsqrt(ms + EPS) * g_ref[...]).astype(BF16)
    o_ref[...] = 0.5 + 0.5 * jnp.tanh(0.5 * _dot(hn_ref[...], wg_ref[...]))
    emb = _dot(p_ref[...].astype(BF16), wp_ref[...])
    out = h_ref[...] + emb * o_ref[...]
    if final_norm:
        out = out * lax.rsqrt(jnp.mean(out * out, axis=-1, keepdims=True) + EPS) * fg_ref[...]
    o_ref[...] = out


def _ple(h, gain, w_gate, p, w_proj, layer, final_gain, *, final_norm, tm=1024):
    m, d = h.shape
    e = p.shape[2]
    tm = min(tm, m)
    assert m % tm == 0
    once = pl.Buffered(1)
    return pl.pallas_call(
        functools.partial(_ple_kernel, final_norm=final_norm),
        grid=(m // tm,),
        in_specs=[pl.BlockSpec((tm, d), lambda i: (i, 0)),
                  pl.BlockSpec((1, d), lambda i: (0, 0)),
                  pl.BlockSpec((None, d, d), lambda i: (layer, 0, 0), pipeline_mode=once),
                  pl.BlockSpec((None, tm, e), lambda i: (layer, i, 0)),
                  pl.BlockSpec((None, e, d), lambda i: (layer, 0, 0), pipeline_mode=once),
                  pl.BlockSpec((1, d), lambda i: (0, 0))],
        out_specs=pl.BlockSpec((tm, d), lambda i: (i, 0)),
        out_shape=jax.ShapeDtypeStruct((m, d), F32),
        scratch_shapes=[pltpu.VMEM((tm, d), BF16)],
        compiler_params=_cparams(("parallel",)),
        name="ple",
    )(h, gain.reshape(1, d), w_gate, p, w_proj, final_gain.reshape(1, d))


def _gdn_gates_kernel(ba_ref, alog_ref, dt_ref, pieces_ref, rows_ref, *, n_heads):
    ba = ba_ref[0]
    lane = lax.broadcasted_iota(jnp.int32, ba.shape, 1)
    tt = ba.shape[0]
    beta = jax.nn.sigmoid(ba)
    g = -jnp.exp(alog_ref[...]) * jax.nn.softplus(ba + dt_ref[...]) * LOG2E
    row = lax.broadcasted_iota(jnp.int32, (tt, tt), 0)
    col = lax.broadcasted_iota(jnp.int32, (tt, tt), 1)
    tri = jnp.where((row // CHUNK == col // CHUNK) & (col <= row), 1.0, 0.0).astype(F32)
    gc = jnp.dot(tri, g, preferred_element_type=F32, precision=HI)
    cols = jnp.where(lane < n_heads, beta, gc)
    rows_ref[0] = cols.T
    hi = cols.astype(BF16).astype(F32)
    rest = cols - hi
    mid = rest.astype(BF16).astype(F32)
    lo = rest - mid
    slot = LANES // 2
    first = jnp.where(lane < slot, hi, pltpu.roll(mid, slot, 1))
    second = jnp.where(lane < slot, lo, 0.0)
    pieces_ref[0] = jnp.concatenate([first, second], axis=1).astype(BF16)


def _gdn_gates(ba, a_log, dt_bias, *, tt=256):
    b, t, _ = ba.shape
    hv = a_log.shape[0]
    assert 4 * hv <= LANES and t % tt == 0 and tt % CHUNK == 0
    pad = lambda v: jnp.zeros((1, LANES), F32).at[0, hv:2 * hv].set(v.astype(F32))
    return pl.pallas_call(
        functools.partial(_gdn_gates_kernel, n_heads=hv),
        grid=(b, t // tt),
        in_specs=[pl.BlockSpec((1, tt, LANES), lambda bi, i: (bi, i, 0)),
                  pl.BlockSpec((1, LANES), lambda bi, i: (0, 0)),
                  pl.BlockSpec((1, LANES), lambda bi, i: (0, 0))],
        out_specs=[pl.BlockSpec((1, tt, 2 * LANES), lambda bi, i: (bi, i, 0)),
                   pl.BlockSpec((1, LANES, tt), lambda bi, i: (bi, 0, i))],
        out_shape=[jax.ShapeDtypeStruct((b, t, 2 * LANES), BF16),
                   jax.ShapeDtypeStruct((b, LANES, t), F32)],
        compiler_params=_cparams(("parallel", "parallel")),
        name="gdn_gates",
    )(ba, pad(a_log), pad(dt_bias))


def _gdn_core_kernel(q_ref, k_ref, v_ref, z_ref, wq_ref, wk_ref, wv_ref, gcol_ref, grow_ref, nw_ref, sel_ref,
                     o_ref, s_ref, qbuf, kbuf, vbuf, *, n_vheads):
    tt = q_ref.shape[1]
    n_g = q_ref.shape[2] // HEAD
    n_e = 2 * n_g
    hv0 = n_e * pl.program_id(1)

    @pl.when(pl.program_id(2) == 0)
    def _():
        s_ref[...] = jnp.zeros_like(s_ref)
        qbuf[...] = jnp.zeros_like(qbuf)
        kbuf[...] = jnp.zeros_like(kbuf)
        vbuf[...] = jnp.zeros_like(vbuf)

    def conv_silu(x_ref, w_ref, buf):
        xe = jnp.concatenate([buf[...], x_ref[0].astype(F32)], axis=0)
        w = w_ref[...]
        acc = w[0:1, :] * xe
        for j in range(1, CONV_K):
            acc = pltpu.roll(acc, 1, 0) + w[j:j + 1, :] * xe
        buf[...] = xe[tt:tt + HALO, :]
        return _silu(acc[HALO:, :])

    def l2norm_heads(x, scale):
        out = []
        for g in range(n_g):
            xg = x[:, g * HEAD:(g + 1) * HEAD]
            out.append(xg * (lax.rsqrt(jnp.sum(xg * xg, axis=-1, keepdims=True) + 1e-6) * scale))
        return out

    q = l2norm_heads(conv_silu(q_ref, wq_ref, qbuf), HEAD ** -0.5)
    k = l2norm_heads(conv_silu(k_ref, wk_ref, kbuf), 1.0)
    v = conv_silu(v_ref, wv_ref, vbuf)
    nw = nw_ref[...]

    ci = lax.broadcasted_iota(jnp.int32, (CHUNK, CHUNK), 0)
    cj = lax.broadcasted_iota(jnp.int32, (CHUNK, CHUNK), 1)
    causal = cj <= ci
    strict = cj < ci
    eye = jnp.where(ci == cj, 1.0, 0.0).astype(F32)
    diag_blk = (ci // SUB) == (cj // SUB)
    level_masks = []
    size = SUB
    while size < CHUNK:
        level_masks.append(((ci // (2 * size)) == (cj // (2 * size))) & ((ci // size) != (cj // size)))
        size *= 2
    n_chunks = tt // CHUNK
    units = [(c, e) for c in range(n_chunks) for e in range(n_e)]

    gates_b = _dot(gcol_ref[0], sel_ref[0])
    gc_rows = [grow_ref[0, pl.ds(n_vheads + hv0 + e, 1), :] for e in range(n_e)]

    q16 = [x.astype(BF16) for x in q]
    k16 = [x.astype(BF16) for x in k]
    qkk = {}
    for c in range(n_chunks):
        r = slice(c * CHUNK, (c + 1) * CHUNK)
        for g in range(n_g):
            qkk[c, g] = _dot_nt(jnp.concatenate([q16[g][r], k16[g][r]], axis=0), k16[g][r])

    p, tinv, off16, qk_m, rhs16, kdec16, qdec, gl = {}, {}, {}, {}, {}, {}, {}, {}
    for (c, e) in units:
        r = slice(c * CHUNK, (c + 1) * CHUNK)
        g = e // 2
        bcol = gates_b[r, e * LANES:(e + 1) * LANES]
        gcol = gates_b[r, (n_e + e) * LANES:(n_e + e + 1) * LANES]
        diff = gcol[:, :CHUNK] - gc_rows[e][:, r]
        decay = jnp.exp2(jnp.where(causal, diff, NEG))
        a = jnp.where(strict, qkk[c, g][CHUNK:] * bcol[:, :CHUNK] * decay, 0.0)
        qk_m[c, e] = (qkk[c, g][:CHUNK] * decay).astype(BF16)
        p[c, e] = -jnp.where(diag_blk, a, 0.0)
        tinv[c, e] = eye + p[c, e]
        off16[c, e] = [jnp.where(m, a, 0.0).astype(BF16) for m in level_masks]
        eg = jnp.exp2(gcol)
        kb = k[g][r] * bcol
        ve = v[r, e * HEAD:(e + 1) * HEAD]
        rhs16[c, e] = jnp.concatenate([kb * eg, ve * bcol], axis=1).astype(BF16)
        glast = gcol[CHUNK - 1:CHUNK, :]
        kdec16[c, e] = (k[g][r] * jnp.exp2(glast - gcol)).astype(BF16)
        qdec[c, e] = q[g][r] * eg
        gl[c, e] = jnp.exp2(glast)
    n_sq = int(math.log2(SUB)) - 1
    for un in units:
        p16 = p[un].astype(BF16)
        p[un] = _dot(p16, p16)
    for _ in range(n_sq - 1):
        for un in units:
            p16 = p[un].astype(BF16)
            tp = _dot(jnp.concatenate([tinv[un].astype(BF16), p16], axis=0), p16)
            tinv[un] = tinv[un] + tp[:CHUNK]
            p[un] = tp[CHUNK:]
    for un in units:
        tinv[un] = tinv[un] + _dot(tinv[un].astype(BF16), p[un].astype(BF16))
    for lvl in range(len(level_masks)):
        nmat = {}
        for un in units:
            nmat[un] = _dot(tinv[un].astype(BF16), off16[un][lvl]).astype(BF16)
        for un in units:
            tinv[un] = tinv[un] - _dot(nmat[un], tinv[un].astype(BF16))
    wu16 = {}
    for un in units:
        wu16[un] = _dot(tinv[un].astype(BF16), rhs16[un]).astype(BF16)
    lhs16, n_loc, o_loc = {}, {}, {}
    for un in units:
        qk_wu = _dot(qk_m[un], wu16[un])
        kd_wu = _dot_tn(kdec16[un], wu16[un])
        lhs16[un] = jnp.concatenate([kd_wu[:, :HEAD], qdec[un] - qk_wu[:, :HEAD]], axis=0).astype(BF16)
        n_loc[un] = kd_wu[:, HEAD:]
        o_loc[un] = qk_wu[:, HEAD:]

    s = [s_ref[e] for e in range(n_e)]
    for c in range(n_chunks):
        r = slice(c * CHUNK, (c + 1) * CHUNK)
        rs = [_dot(lhs16[c, e], s[e].astype(BF16)) for e in range(n_e)]
        for e in range(n_e):
            o = rs[e][HEAD:] + o_loc[c, e]
            s[e] = s[e] * gl[c, e] - rs[e][:HEAD] + n_loc[c, e]
            on = o * lax.rsqrt(jnp.mean(o * o, axis=-1, keepdims=True) + EPS) * nw
            ze = z_ref[0, r, e * HEAD:(e + 1) * HEAD].astype(F32)
            o_ref[0, r, e * HEAD:(e + 1) * HEAD] = (on * _silu(ze)).astype(o_ref.dtype)
    for e in range(n_e):
        s_ref[e] = s[e]


def _gdn_core(proj, conv_w, gpieces, grows, norm_w, *, n_qheads, n_vheads, tt=256, group=4):
    b, t, _ = proj.shape
    group = min(group, n_qheads)
    assert n_vheads == 2 * n_qheads and n_qheads % group == 0 and t % tt == 0 and tt % CHUNK == 0
    qk_dim = n_qheads * HEAD
    v_dim = n_vheads * HEAD
    qw, vw = group * HEAD, 2 * group * HEAD
    n_groups = n_qheads // group
    assert (2 * qk_dim) % vw == 0 and (2 * qk_dim + v_dim) % vw == 0
    v_blk0 = (2 * qk_dim) // vw
    z_blk0 = (2 * qk_dim + v_dim) // vw
    n_e = 2 * group
    slot = LANES // 2
    sel = np.zeros((n_groups, 2 * LANES, 2 * n_e * LANES), np.float32)
    for gi in range(n_groups):
        for e in range(n_e):
            for piece in range(3):
                sel[gi, piece * slot + gi * n_e + e, e * LANES:(e + 1) * LANES] = 1.0
                sel[gi, piece * slot + n_vheads + gi * n_e + e, (n_e + e) * LANES:(n_e + e + 1) * LANES] = 1.0
    return pl.pallas_call(
        functools.partial(_gdn_core_kernel, n_vheads=n_vheads),
        grid=(b, n_groups, t // tt),
        in_specs=[pl.BlockSpec((1, tt, qw), lambda bi, h, ti: (bi, ti, h)),
                  pl.BlockSpec((1, tt, qw), lambda bi, h, ti: (bi, ti, n_groups + h)),
                  pl.BlockSpec((1, tt, vw), lambda bi, h, ti: (bi, ti, v_blk0 + h)),
                  pl.BlockSpec((1, tt, vw), lambda bi, h, ti: (bi, ti, z_blk0 + h)),
                  pl.BlockSpec((CONV_K, qw), lambda bi, h, ti: (0, h)),
                  pl.BlockSpec((CONV_K, qw), lambda bi, h, ti: (0, n_groups + h)),
                  pl.BlockSpec((CONV_K, vw), lambda bi, h, ti: (0, v_blk0 + h)),
                  pl.BlockSpec((1, tt, 2 * LANES), lambda bi, h, ti: (bi, ti, 0)),
                  pl.BlockSpec((1, LANES, tt), lambda bi, h, ti: (bi, 0, ti)),
                  pl.BlockSpec((1, HEAD), lambda bi, h, ti: (0, 0)),
                  pl.BlockSpec((1, 2 * LANES, 2 * n_e * LANES), lambda bi, h, ti: (h, 0, 0))],
        out_specs=pl.BlockSpec((1, tt, vw), lambda bi, h, ti: (bi, ti, h)),
        out_shape=jax.ShapeDtypeStruct((b, t, v_dim), BF16),
        scratch_shapes=[pltpu.VMEM((2 * group, HEAD, HEAD), F32),
                        pltpu.VMEM((HALO, qw), F32),
                        pltpu.VMEM((HALO, qw), F32),
                        pltpu.VMEM((HALO, vw), F32)],
        compiler_params=_cparams(("parallel", "parallel", "arbitrary")),
        name="gdn_core",
    )(proj, proj, proj, proj, conv_w, conv_w, conv_w, gpieces, grows, norm_w.reshape(1, HEAD),
      jnp.asarray(sel, BF16))


def _diff_attn_kernel(qi_tab, ki_tab, q1_ref, q2_ref, k1_ref, k2_ref, v_ref, lam_ref, sw_ref, o_ref,
                      m_ref, l_ref, acc_ref, *, n_heads, out_scale, lam_init):
    step = pl.program_id(2)
    h = pl.program_id(1)
    qi = qi_tab[step]
    ki = ki_tab[step]
    tq = q1_ref.shape[1]
    tk = k1_ref.shape[1]

    @pl.when(ki == 0)
    def _():
        m_ref[...] = jnp.full_like(m_ref, NEG)
        l_ref[...] = jnp.zeros_like(l_ref)
        acc_ref[...] = jnp.zeros_like(acc_ref)

    slope = jnp.exp2(-8.0 * (h + 1).astype(F32) / n_heads) * LOG2E
    off = ki * tk - qi * tq
    kpos = (lax.broadcasted_iota(jnp.int32, (1, tk), 1) + off).astype(F32)
    bias = slope * kpos

    def update(masked):
        v16 = v_ref[0]
        scores = [_dot_nt(q1_ref[0], k1_ref[0]) + bias, _dot_nt(q2_ref[0], k2_ref[0]) + bias]
        if masked:
            row = lax.broadcasted_iota(jnp.int32, (tq, tk), 0)
            col = lax.broadcasted_iota(jnp.int32, (tq, tk), 1)
            visible = (col + off) <= row
            scores = [jnp.where(visible, s, NEG) for s in scores]
        for c, s in enumerate(scores):
            m_old = m_ref[c]
            m_new = jnp.maximum(m_old, jnp.max(s, axis=-1, keepdims=True))
            p = jnp.exp2(s - jnp.concatenate([m_new] * (tk // LANES), axis=1))
            alpha = jnp.exp2(m_old - m_new)
            l_ref[c] = alpha * l_ref[c] + jnp.sum(p, axis=-1, keepdims=True)
            acc_ref[c] = (jnp.concatenate([alpha] * (acc_ref.shape[2] // LANES), axis=1) * acc_ref[c]
                          + _dot(p.astype(BF16), v16))
            m_ref[c] = m_new

    touches_diag = (ki + 1) * tk > qi * tq + 1
    pl.when(touches_diag)(functools.partial(update, True))
    pl.when(jnp.logical_not(touches_diag))(functools.partial(update, False))

    @pl.when((ki + 1) * tk >= (qi + 1) * tq)
    def _():
        lq1, lk1, lq2, lk2 = lam_ref[0:1, :], lam_ref[1:2, :], lam_ref[2:3, :], lam_ref[3:4, :]
        lam = (jnp.exp(jnp.sum(lq1 * lk1, axis=-1, keepdims=True))
               - jnp.exp(jnp.sum(lq2 * lk2, axis=-1, keepdims=True)) + lam_init)
        rep = acc_ref.shape[2] // LANES
        inv1 = jnp.concatenate([1.0 / l_ref[0]] * rep, axis=1)
        inv2 = jnp.concatenate([1.0 / l_ref[1]] * rep, axis=1)
        o = acc_ref[0] * inv1 - lam * (acc_ref[1] * inv2)
        on = o * lax.rsqrt(jnp.mean(o * o, axis=-1, keepdims=True) + EPS) * sw_ref[...]
        o_ref[0] = (on * out_scale).astype(o_ref.dtype)


def _diff_attn(q, kv, lam_vecs, subln_w, *, n_heads, layer_idx, tq=512, tk=512):
    b, t, qw = q.shape
    tq, tk = min(tq, t), min(tk, t)
    assert t % tq == 0 and tq % tk == 0
    dv = 2 * HEAD
    v_blk0 = qw // dv
    pairs = [(i, j) for i in range(t // tq) for j in range(((i + 1) * tq) // tk)]
    qi_tab = jnp.asarray(np.array([p[0] for p in pairs], np.int32))
    ki_tab = jnp.asarray(np.array([p[1] for p in pairs], np.int32))
    lam_init = 0.8 - 0.6 * math.exp(-0.3 * layer_idx)
    grid_spec = pltpu.PrefetchScalarGridSpec(
        num_scalar_prefetch=2,
        grid=(b, n_heads, len(pairs)),
        in_specs=[pl.BlockSpec((1, tq, HEAD), lambda bi, h, s, qt, kt: (bi, qt[s], 2 * h)),
                  pl.BlockSpec((1, tq, HEAD), lambda bi, h, s, qt, kt: (bi, qt[s], 2 * h + 1)),
                  pl.BlockSpec((1, tk, HEAD), lambda bi, h, s, qt, kt: (bi, kt[s], 2 * h)),
                  pl.BlockSpec((1, tk, HEAD), lambda bi, h, s, qt, kt: (bi, kt[s], 2 * h + 1)),
                  pl.BlockSpec((1, tk, dv), lambda bi, h, s, qt, kt: (bi, kt[s], v_blk0 + h)),
                  pl.BlockSpec((4, HEAD), lambda bi, h, s, qt, kt: (0, 0)),
                  pl.BlockSpec((1, dv), lambda bi, h, s, qt, kt: (0, 0))],
        out_specs=pl.BlockSpec((1, tq, dv), lambda bi, h, s, qt, kt: (bi, qt[s], h)),
        scratch_shapes=[pltpu.VMEM((2, tq, LANES), F32),
                        pltpu.VMEM((2, tq, LANES), F32),
                        pltpu.VMEM((2, tq, dv), F32)])
    return pl.pallas_call(
        functools.partial(_diff_attn_kernel, n_heads=n_heads, out_scale=1.0 - lam_init, lam_init=lam_init),
        grid_spec=grid_spec,
        out_shape=jax.ShapeDtypeStruct((b, t, n_heads * dv), BF16),
        compiler_params=_cparams(("parallel", "parallel", "arbitrary")),
        name="diff_attn",
    )(qi_tab, ki_tab, q, q, kv, kv, kv, lam_vecs, subln_w.reshape(1, dv))


def kernel(x, p, norm_mix, norm_mlp, norm_ple, gdn_w_in, gdn_conv_w, gdn_a_log, gdn_dt_bias, gdn_norm_w, gdn_w_out, kv_norm, w_kv, diff_w_q, diff_lambda_q1, diff_lambda_k1, diff_lambda_q2, diff_lambda_k2, diff_subln_w, diff_w_o, mlp_w1, mlp_w2, ple_w_proj, ple_w_gate, final_norm):
    b, t, d = x.shape
    m = b * t
    depth = norm_mix.shape[0]
    n_a = gdn_w_in.shape[0]
    n_vheads = gdn_a_log.shape[1]
    v_dim = n_vheads * HEAD
    conv_dim = gdn_conv_w.shape[2]
    qk_dim = (conv_dim - v_dim) // 2
    n_qheads = qk_dim // HEAD
    main_w = conv_dim + v_dim
    n_dheads = diff_w_q.shape[2] // (2 * HEAD)

    assert main_w % LANES == 0
    h = x.reshape(m, d)
    p2 = p.reshape(depth, m, -1)
    gdn_w_in, gdn_w_out, w_kv, diff_w_q, diff_w_o, mlp_w1, mlp_w2, ple_w_gate, ple_w_proj = (
        w.astype(BF16) for w in (gdn_w_in, gdn_w_out, w_kv, diff_w_q, diff_w_o, mlp_w1, mlp_w2,
                                 ple_w_gate, ple_w_proj))
    kv = None
    for i in range(depth):
        if i < n_a:
            proj = _norm_matmul(h, norm_mix[i], gdn_w_in, i, n=main_w)
            ba = _norm_matmul(h, norm_mix[i], gdn_w_in, i, n=LANES, col0=main_w // LANES, n_valid=2 * n_vheads,
                              out_dtype=F32)
            gpieces, grows = _gdn_gates(ba.reshape(b, t, LANES), gdn_a_log[i], gdn_dt_bias[i])
            o = _gdn_core(proj.reshape(b, t, main_w), gdn_conv_w[i], gpieces, grows,
                          gdn_norm_w[i], n_qheads=n_qheads, n_vheads=n_vheads)
            h = _matmul_residual(o.reshape(m, v_dim), gdn_w_out, i, h)
        else:
            j = i - n_a
            q = _norm_matmul(h, norm_mix[i], diff_w_q, j, scale=HEAD ** -0.5 * LOG2E)
            lam_vecs = jnp.stack([diff_lambda_q1[j], diff_lambda_k1[j],
                                  diff_lambda_q2[j], diff_lambda_k2[j]]).astype(F32)
            o = _diff_attn(q.reshape(b, t, -1), kv, lam_vecs, diff_subln_w[j],
                           n_heads=n_dheads, layer_idx=i)
            h = _matmul_residual(o.reshape(m, -1), diff_w_o, j, h)
        a = _norm_matmul(h, norm_mlp[i], mlp_w1, i, act="relu2")
        h = _matmul_residual(a, mlp_w2, i, h)
        h = _ple(h, norm_ple[i], ple_w_gate, p2, ple_w_proj, i, final_norm, final_norm=(i == depth - 1))
        if i == n_a - 1:
            kv = _norm_matmul(h, kv_norm, w_kv[None], 0).reshape(b, t, -1)
    return h.reshape(b, t, d)
```
